```python
import jax, jax.numpy as jnp
from jax import lax
import numpy as np

D_MODEL = 4096
BATCH = 4
SEQ = 2048
DEPTH = 4
DEC_BATCH = 8
DEC_SEQ = 4
PAST_LEN = 8192
PAGE_SIZE = 128

N_MIXERS = 3
GLA_HEADS = 4
GLA_DK = D_MODEL // 2
GLA_DV = D_MODEL
GLA_DK_H = GLA_DK // GLA_HEADS
GLA_DV_H = GLA_DV // GLA_HEADS
GLA_GATE_RANK = 16
GLA_TAU = 16.0
GLA_CHUNK = 32
GLA_IN_WIDTH = 2 * GLA_DK + 2 * GLA_DV + GLA_GATE_RANK
SB_HEADS = 32
SB_HEAD_DIM = D_MODEL // SB_HEADS
SB_BLOCK = 128
SB_SCALE = SB_HEAD_DIM ** -0.5
SB_BIAS_INIT = -6.0
CONV_WIDTH = 3
N_MEM = 256
X_HEADS = 4
X_HEAD_DIM = 128
X_WIDTH = X_HEADS * X_HEAD_DIM
X_SCALE = X_HEAD_DIM ** -0.5
D_FF = -(-8 * D_MODEL // (3 * 256)) * 256
RMS_EPS = 1e-6
N_GLA = (DEPTH + 2) // 3
N_SB = (DEPTH + 1) // 3
N_CONV = DEPTH // 3

kernel_name = 'hybrid_gla_stickbreak_shortconv_decoder_step'


def rmsnorm(x, g):
    xf = x.astype(jnp.float32)
    y = xf * lax.rsqrt(jnp.mean(xf * xf, axis=-1, keepdims=True) + RMS_EPS)
    return (y * g.astype(jnp.float32)).astype(x.dtype)


def gla_recurrence(q, k, v, log_a, s0):
    B, L, H, dk = q.shape
    dv = v.shape[-1]
    c = GLA_CHUNK if L % GLA_CHUNK == 0 else L
    n = L // c

    def blocks(t):
        return t.astype(jnp.float32).reshape(B, n, c, H, t.shape[-1]).transpose(1, 0, 3, 2, 4)

    causal = jnp.tril(jnp.ones((c, c), dtype=bool))

    def step(s, inp):
        qc, kc, vc, gc = inp
        b = jnp.cumsum(gc, axis=2)
        b_last = b[:, :, -1:, :]
        q_dec = qc * jnp.exp(b)
        att = jnp.einsum('bhtk,bhsk->bhts', q_dec, kc * jnp.exp(-b))
        att = jnp.where(causal, att, 0.0)
        o = jnp.einsum('bhtk,bhkv->bhtv', q_dec, s) + jnp.einsum('bhts,bhsv->bhtv', att, vc)
        k_dec = kc * jnp.exp(b_last - b)
        s = s * jnp.exp(b_last)[:, :, 0, :, None] + jnp.einsum('bhsk,bhsv->bhkv', k_dec, vc)
        return s, o

    s, o = lax.scan(step, s0.astype(jnp.float32), (blocks(q), blocks(k), blocks(v), blocks(log_a)))
    o = o.transpose(1, 0, 3, 2, 4).reshape(B, L, H, dv)
    return o, s


def gla_mixer(xn, s0, w_in, w_gate, b_gate, g_head, w_out):
    B, L, _ = xn.shape
    proj = xn @ w_in
    q, k, v, r, g1 = jnp.split(proj, [GLA_DK, 2 * GLA_DK, 2 * GLA_DK + GLA_DV, 2 * GLA_DK + 2 * GLA_DV], axis=-1)
    log_a = jax.nn.log_sigmoid((g1 @ w_gate + b_gate).astype(jnp.float32)) / GLA_TAU
    q = q.reshape(B, L, GLA_HEADS, GLA_DK_H) * (GLA_DK_H ** -0.5)
    k = k.reshape(B, L, GLA_HEADS, GLA_DK_H)
    v = v.reshape(B, L, GLA_HEADS, GLA_DV_H)
    log_a = log_a.reshape(B, L, GLA_HEADS, GLA_DK_H)
    o, s = gla_recurrence(q, k, v, log_a, s0)
    o = rmsnorm(o.astype(xn.dtype), g_head)
    o = o.reshape(B, L, GLA_DV) * jax.nn.silu(r)
    return o @ w_out, s


def sb_weights(z, bias, visible):
    z = z.astype(jnp.float32) * SB_SCALE + bias.astype(jnp.float32)[None, :, None, None]
    log_keep = jnp.where(visible, -jax.nn.softplus(z), 0.0)
    later = lax.cumsum(log_keep, axis=z.ndim - 1, reverse=True) - log_keep
    return jnp.where(visible, jnp.exp(later - jax.nn.softplus(-z)), 0.0)


def sb_project(xn, w_in):
    B, L, _ = xn.shape
    q, k, v = jnp.split(xn @ w_in, 3, axis=-1)
    shp = (B, L, SB_HEADS, SB_HEAD_DIM)
    return q.reshape(shp), k.reshape(shp), v.reshape(shp)


def sb_prompt(q, k, v, bias):
    B, S, H, Dh = q.shape
    nb = S // SB_BLOCK
    q_blocks = q.reshape(B, nb, SB_BLOCK, H, Dh).transpose(1, 0, 2, 3, 4)
    q_pos = jnp.arange(S).reshape(nb, SB_BLOCK)
    k_pos = jnp.arange(S)

    def block(args):
        qb, qp = args
        z = jnp.einsum('bqhd,bkhd->bhqk', qb, k)
        w = sb_weights(z, bias, k_pos[None, :] < qp[:, None])
        return jnp.einsum('bhqk,bkhd->bqhd', w.astype(v.dtype), v)

    o = lax.map(block, (q_blocks, q_pos))
    return o.transpose(1, 0, 2, 3, 4).reshape(B, S, H * Dh)


def sb_sample(q, k, v, k_past, v_past, bias):
    Bd, T, H, Dh = q.shape
    past = k_past.shape[1]
    z = jnp.concatenate([jnp.einsum('bqhd,bkhd->bhqk', q, k_past),
                         jnp.einsum('bqhd,bkhd->bhqk', q, k)], axis=-1)
    q_pos = past + jnp.arange(T)
    k_pos = jnp.arange(past + T)
    w = sb_weights(z, bias, k_pos[None, :] < q_pos[:, None]).astype(v.dtype)
    o = (jnp.einsum('bhqk,bkhd->bqhd', w[..., :past], v_past)
         + jnp.einsum('bhqk,bkhd->bqhd', w[..., past:], v))
    return o.reshape(Bd, T, H * Dh)


def conv_mixer(xn, buf0, w_in, w_conv, w_out):
    bg, cg, h = jnp.split(xn @ w_in, 3, axis=-1)
    u = cg * h
    L = u.shape[1]
    buf = jnp.concatenate([buf0.astype(u.dtype), u], axis=1)
    conv = w_conv[0] * buf[:, 0:L]
    for i in range(1, CONV_WIDTH):
        conv = conv + w_conv[i] * buf[:, i:i + L]
    return (bg * conv) @ w_out, buf[:, L:]


def mem_kv(mem, g_mem, w_kv):
    B, M, _ = mem.shape
    k, v = jnp.split(rmsnorm(mem, g_mem) @ w_kv, 2, axis=-1)
    return k.reshape(B, M, X_HEADS, X_HEAD_DIM), v.reshape(B, M, X_HEADS, X_HEAD_DIM)


def cross_attend(xn, mk, mv, w_q, w_o):
    B, L, _ = xn.shape
    q = (xn @ w_q).reshape(B, L, X_HEADS, X_HEAD_DIM)
    s = jnp.einsum('blhd,bmhd->bhlm', q, mk.astype(q.dtype)).astype(jnp.float32) * X_SCALE
    p = jax.nn.softmax(s, axis=-1).astype(xn.dtype)
    o = jnp.einsum('bhlm,bmhd->blhd', p, mv.astype(xn.dtype)).reshape(B, L, X_WIDTH)
    return o @ w_o


def swiglu(xn, w_in, w_out):
    gate, up = jnp.split(xn @ w_in, 2, axis=-1)
    return (jax.nn.silu(gate) * up) @ w_out


def setup_inputs(seed: int = 0) -> dict:
    key = jax.random.key(seed)
    ks = jax.random.split(key, 32)
    f32 = jnp.float32
    n_pages = PAST_LEN // PAGE_SIZE
    n_used = DEC_BATCH * n_pages
    n_pool = n_used + max(n_used // 4, 1)

    def normal(k, shape, scale=1.0):
        return jax.random.normal(k, shape, f32) * scale

    def gain(k, shape):
        return 1.0 + 0.02 * normal(k, shape)

    page_table = jax.random.permutation(ks[6], n_pool)[:n_used].reshape(DEC_BATCH, n_pages).astype(jnp.int32)
    return {
        'x_prompt': normal(ks[0], (BATCH, SEQ, D_MODEL)),
        'x_sample': normal(ks[1], (DEC_BATCH, DEC_SEQ, D_MODEL)),
        'state_gla': normal(ks[2], (N_GLA, DEC_BATCH, GLA_HEADS, GLA_DK_H, GLA_DV_H), 0.5),
        'cache_sb_k': normal(ks[3], (N_SB, n_pool, PAGE_SIZE, SB_HEADS, SB_HEAD_DIM)),
        'cache_sb_v': normal(ks[4], (N_SB, n_pool, PAGE_SIZE, SB_HEADS, SB_HEAD_DIM)),
        'state_conv': normal(ks[5], (N_CONV, DEC_BATCH, CONV_WIDTH - 1, D_MODEL)),
        'cache_mem_k': normal(ks[7], (DEPTH, DEC_BATCH, N_MEM, X_HEADS, X_HEAD_DIM)),
        'cache_mem_v': normal(ks[8], (DEPTH, DEC_BATCH, N_MEM, X_HEADS, X_HEAD_DIM)),
        'page_table': page_table,
        'mem_prompt': normal(ks[9], (BATCH, N_MEM, D_MODEL)),
        'g_mix': gain(ks[10], (DEPTH, D_MODEL)),
        'g_xattn': gain(ks[11], (DEPTH, D_MODEL)),
        'g_mem': gain(ks[12], (DEPTH, D_MODEL)),
        'g_ffn': gain(ks[13], (DEPTH, D_MODEL)),
        'g_final': gain(ks[14], (D_MODEL,)),
        'w_gla_in': normal(ks[15], (N_GLA, D_MODEL, GLA_IN_WIDTH), D_MODEL ** -0.5),
        'w_gla_gate': normal(ks[16], (N_GLA, GLA_GATE_RANK, GLA_DK), GLA_GATE_RANK ** -0.5),
        'b_gla_gate': normal(ks[17], (N_GLA, GLA_DK), 0.1),
        'g_gla_head': gain(ks[18], (N_GLA, GLA_DV_H)),
        'w_gla_out': normal(ks[19], (N_GLA, GLA_DV, D_MODEL), GLA_DV ** -0.5),
        'w_sb_in': normal(ks[20], (N_SB, D_MODEL, 3 * D_MODEL), D_MODEL ** -0.5),
        'b_sb': SB_BIAS_INIT + normal(ks[30], (N_SB, SB_HEADS), 0.1),
        'w_sb_out': normal(ks[21], (N_SB, D_MODEL, D_MODEL), D_MODEL ** -0.5),
        'w_conv_in': normal(ks[22], (N_CONV, D_MODEL, 3 * D_MODEL), D_MODEL ** -0.5),
        'w_conv': normal(ks[23], (N_CONV, CONV_WIDTH, D_MODEL), CONV_WIDTH ** -0.5),
        'w_conv_out': normal(ks[24], (N_CONV, D_MODEL, D_MODEL), D_MODEL ** -0.5),
        'w_xq': normal(ks[25], (DEPTH, D_MODEL, X_WIDTH), D_MODEL ** -0.5),
        'w_xkv': normal(ks[26], (DEPTH, D_MODEL, 2 * X_WIDTH), D_MODEL ** -0.5),
        'w_xo': normal(ks[27], (DEPTH, X_WIDTH, D_MODEL), X_WIDTH ** -0.5),
        'w_ffn_in': normal(ks[28], (DEPTH, D_MODEL, 2 * D_FF), D_MODEL ** -0.5),
        'w_ffn_out': normal(ks[29], (DEPTH, D_FF, D_MODEL), D_FF ** -0.5),
    }


def reference(x_prompt, x_sample, state_gla, cache_sb_k, cache_sb_v, state_conv, cache_mem_k, cache_mem_v,
              page_table, mem_prompt, g_mix, g_xattn, g_mem, g_ffn, g_final,
              w_gla_in, w_gla_gate, b_gla_gate, g_gla_head, w_gla_out,
              w_sb_in, b_sb, w_sb_out, w_conv_in, w_conv, w_conv_out,
              w_xq, w_xkv, w_xo, w_ffn_in, w_ffn_out):
    B, S, _ = x_prompt.shape
    Bd, T, _ = x_sample.shape
    past = page_table.shape[1] * cache_sb_k.shape[2]
    hp, hs = x_prompt, x_sample
    gla_p, gla_s = [], []
    sbk_p, sbv_p, sbk_s, sbv_s = [], [], [], []
    conv_p, conv_s = [], []
    memk_p, memv_p = [], []
    for i in range(DEPTH):
        kind, j = i % N_MIXERS, i // N_MIXERS
        xp = rmsnorm(hp, g_mix[i])
        xs = rmsnorm(hs, g_mix[i])
        if kind == 0:
            s0 = jnp.zeros((B, GLA_HEADS, GLA_DK_H, GLA_DV_H), jnp.float32)
            yp, sp = gla_mixer(xp, s0, w_gla_in[j], w_gla_gate[j], b_gla_gate[j], g_gla_head[j], w_gla_out[j])
            ys, ss = gla_mixer(xs, state_gla[j], w_gla_in[j], w_gla_gate[j], b_gla_gate[j], g_gla_head[j], w_gla_out[j])
            gla_p.append(sp.astype(state_gla.dtype))
            gla_s.append(ss.astype(state_gla.dtype))
        elif kind == 1:
            qp, kp, vp = sb_project(xp, w_sb_in[j])
            yp = sb_prompt(qp, kp, vp, b_sb[j]) @ w_sb_out[j]
            qs, kn, vn = sb_project(xs, w_sb_in[j])
            k_past = cache_sb_k[j][page_table].reshape(Bd, past, SB_HEADS, SB_HEAD_DIM).astype(kn.dtype)
            v_past = cache_sb_v[j][page_table].reshape(Bd, past, SB_HEADS, SB_HEAD_DIM).astype(vn.dtype)
            ys = sb_sample(qs, kn, vn, k_past, v_past, b_sb[j]) @ w_sb_out[j]
            sbk_p.append(kp)
            sbv_p.append(vp)
            sbk_s.append(kn)
            sbv_s.append(vn)
        else:
            buf0 = jnp.zeros((B, CONV_WIDTH - 1, D_MODEL), xp.dtype)
            yp, bp = conv_mixer(xp, buf0, w_conv_in[j], w_conv[j], w_conv_out[j])
            ys, bs = conv_mixer(xs, state_conv[j], w_conv_in[j], w_conv[j], w_conv_out[j])
            conv_p.append(bp)
            conv_s.append(bs)
        hp = hp + yp
        hs = hs + ys
        mkp, mvp = mem_kv(mem_prompt, g_mem[i], w_xkv[i])
        memk_p.append(mkp)
        memv_p.append(mvp)
        hp = hp + cross_attend(rmsnorm(hp, g_xattn[i]), mkp, mvp, w_xq[i], w_xo[i])
        hs = hs + cross_attend(rmsnorm(hs, g_xattn[i]), cache_mem_k[i], cache_mem_v[i], w_xq[i], w_xo[i])
        hp = hp + swiglu(rmsnorm(hp, g_ffn[i]), w_ffn_in[i], w_ffn_out[i])
        hs = hs + swiglu(rmsnorm(hs, g_ffn[i]), w_ffn_in[i], w_ffn_out[i])
    y_prompt = rmsnorm(hp, g_final)
    y_sample = rmsnorm(hs, g_final)
    return (y_prompt, y_sample,
            jnp.stack(gla_p), jnp.stack(gla_s),
            jnp.stack(sbk_p), jnp.stack(sbv_p), jnp.stack(sbk_s), jnp.stack(sbv_s),
            jnp.stack(conv_p), jnp.stack(conv_s),
            jnp.stack(memk_p), jnp.stack(memv_p))
```

```python
import functools

import jax
import jax.numpy as jnp
from jax import lax
from jax.experimental import pallas as pl
from jax.experimental.pallas import tpu as pltpu

F32 = jnp.float32
BF16 = jnp.bfloat16

LANES_V7X = 128
VMEM_LIMIT_V7X = 56 * 1024 * 1024
RMS_EPS = 1e-6
GLA_TAU = 16.0
GLA_CHUNK = 32
GLA_GATE_PAD = LANES_V7X
SB_KEY_BLOCK = 128
NEG_BIG = -1e30


def _params(*sem):
    return pltpu.CompilerParams(dimension_semantics=sem, vmem_limit_bytes=VMEM_LIMIT_V7X)


def _tile(n, pref):
    if n <= pref:
        return n
    t = (pref // LANES_V7X) * LANES_V7X
    while t > LANES_V7X and n % t:
        t -= LANES_V7X
    assert n % t == 0, (n, pref)
    return t


def _softplus(z):
    return jnp.maximum(z, 0.0) + jnp.log1p(jnp.exp(-jnp.abs(z)))


def _sigmoid(z):
    return 1.0 / (1.0 + jnp.exp(-z))


def _split3(x):
    hi = x.astype(BF16)
    r1 = x - hi.astype(F32)
    mid = r1.astype(BF16)
    lo = (r1 - mid.astype(F32)).astype(BF16)
    return hi, mid, lo


def _dot(a, b):
    return jnp.dot(a, b, preferred_element_type=F32)


def _dot_nt(a, b):
    return lax.dot_general(a, b, (((1,), (1,)), ((), ())), preferred_element_type=F32)


def _dot_tn(a, b):
    return lax.dot_general(a, b, (((0,), (0,)), ((), ())), preferred_element_type=F32)


def _rmsnorm_body(x_ref, g_ref, o_ref):
    x = x_ref[...]
    ms = jnp.mean(x * x, axis=-1, keepdims=True)
    o_ref[...] = ((x * lax.rsqrt(ms + RMS_EPS)) * g_ref[...]).astype(o_ref.dtype)


def _rmsnorm(x, g, out_dtype):
    m, d = x.shape
    tm = _tile(m, 512)
    return pl.pallas_call(
        _rmsnorm_body,
        grid=(m // tm,),
        in_specs=[pl.BlockSpec((tm, d), lambda i: (i, 0)), pl.BlockSpec((1, d), lambda i: (0, 0))],
        out_specs=pl.BlockSpec((tm, d), lambda i: (i, 0)),
        out_shape=jax.ShapeDtypeStruct((m, d), out_dtype),
        compiler_params=_params("parallel"),
        name="rmsnorm",
    )(x, g.reshape(1, d))


def _mm_body(x_ref, w_ref, o_ref):
    o_ref[...] = _dot(x_ref[...], w_ref[...]).astype(o_ref.dtype)


def _mm_res_body(x_ref, w_ref, r_ref, o_ref):
    o_ref[...] = (_dot(x_ref[...], w_ref[...]) + r_ref[...]).astype(o_ref.dtype)


def _matmul(x, w, *, n_out, col_off=0, res=None, out_dtype=F32, tm_pref=1024, tn_pref=512, name="matmul"):
    m, k = x.shape
    tm, tn = _tile(m, tm_pref), _tile(n_out, tn_pref)
    assert col_off % tn == 0
    off = col_off // tn
    in_specs = [pl.BlockSpec((tm, k), lambda i, j: (i, 0)), pl.BlockSpec((k, tn), lambda i, j: (0, j + off))]
    args = [x, w]
    body = _mm_body
    if res is not None:
        in_specs.append(pl.BlockSpec((tm, tn), lambda i, j: (i, j)))
        args.append(res)
        body = _mm_res_body
    return pl.pallas_call(
        body,
        grid=(m // tm, n_out // tn),
        in_specs=in_specs,
        out_specs=pl.BlockSpec((tm, tn), lambda i, j: (i, j)),
        out_shape=jax.ShapeDtypeStruct((m, n_out), out_dtype),
        compiler_params=_params("parallel", "parallel"),
        name=name,
    )(*args)


def _mem_kv_body(x_ref, g_ref, w_ref, k_ref, v_ref):
    x = x_ref[...]
    ms = jnp.mean(x * x, axis=-1, keepdims=True)
    xn = ((x * lax.rsqrt(ms + RMS_EPS)) * g_ref[0]).astype(BF16)
    kv = _dot(xn, w_ref[0])
    half = kv.shape[1] // 2
    k_ref[0] = kv[:, :half]
    v_ref[0] = kv[:, half:]


def _mem_kv(mem, g_mem, w_xkv_bf):
    rows, d = mem.shape
    depth, _, two_w = w_xkv_bf.shape
    xw = two_w // 2
    tm = _tile(rows, 512)
    out = jax.ShapeDtypeStruct((depth, rows, xw), F32)
    return pl.pallas_call(
        _mem_kv_body,
        grid=(depth, rows // tm),
        in_specs=[
            pl.BlockSpec((tm, d), lambda i, m: (m, 0)),
            pl.BlockSpec((1, 1, d), lambda i, m: (i, 0, 0)),
            pl.BlockSpec((1, d, two_w), lambda i, m: (i, 0, 0)),
        ],
        out_specs=[pl.BlockSpec((1, tm, xw), lambda i, m: (i, m, 0))] * 2,
        out_shape=[out, out],
        compiler_params=_params("parallel", "parallel"),
        name="mem_kv",
    )(mem, g_mem.reshape(depth, 1, d), w_xkv_bf)


def _xattn_body(x_ref, g_ref, wq_ref, mk_ref, mv_ref, wo_ref, o_ref, *, heads, scale, rows_per_group, mem_per_group):
    x = x_ref[...]
    ms = jnp.mean(x * x, axis=-1, keepdims=True)
    xn = ((x * lax.rsqrt(ms + RMS_EPS)) * g_ref[...]).astype(BF16)
    q = _dot(xn, wq_ref[...])
    mk = mk_ref[...].astype(BF16)
    mv = mv_ref[...].astype(BF16)
    hd = q.shape[1] // heads
    outs = []
    for h in range(heads):
        sl = slice(h * hd, (h + 1) * hd)
        s = _dot_nt(q[:, sl].astype(BF16), mk[:, sl]) * scale
        if rows_per_group is not None:
            rg = lax.broadcasted_iota(jnp.int32, s.shape, 0) // rows_per_group
            cg = lax.broadcasted_iota(jnp.int32, s.shape, 1) // mem_per_group
            s = jnp.where(rg == cg, s, NEG_BIG)
        e = jnp.exp(s - jnp.max(s, axis=-1, keepdims=True))
        p = e / jnp.sum(e, axis=-1, keepdims=True)
        outs.append(_dot(p.astype(BF16), mv[:, sl]))
    o = jnp.concatenate(outs, axis=-1).astype(BF16)
    o_ref[...] = _dot(o, wo_ref[...]) + x


def _xattn(h, g, wq_bf, mk, mv, wo_bf, *, heads, rows_per_seq, n_mem, grouped):
    m, d = h.shape
    xw = wq_bf.shape[1]
    scale = (xw // heads) ** -0.5
    if grouped:
        tm, mem_rows = m, mk.shape[0]
        mem_map = lambda i: (0, 0)
        rpg = rows_per_seq
    else:
        tm, mem_rows = _tile(rows_per_seq, 256), n_mem
        blocks_per_seq = rows_per_seq // tm
        mem_map = lambda i: (i // blocks_per_seq, 0)
        rpg = None
    body = functools.partial(_xattn_body, heads=heads, scale=scale, rows_per_group=rpg, mem_per_group=n_mem)
    return pl.pallas_call(
        body,
        grid=(m // tm,),
        in_specs=[
            pl.BlockSpec((tm, d), lambda i: (i, 0)),
            pl.BlockSpec((1, d), lambda i: (0, 0)),
            pl.BlockSpec((d, xw), lambda i: (0, 0)),
            pl.BlockSpec((mem_rows, xw), mem_map),
            pl.BlockSpec((mem_rows, xw), mem_map),
            pl.BlockSpec((xw, d), lambda i: (0, 0)),
        ],
        out_specs=pl.BlockSpec((tm, d), lambda i: (i, 0)),
        out_shape=jax.ShapeDtypeStruct((m, d), F32),
        compiler_params=_params("parallel"),
        name="xattn",
    )(h, g.reshape(1, d), wq_bf, mk, mv, wo_bf)


def _swiglu_in_body(x_ref, wg_ref, wu_ref, o_ref):
    x = x_ref[...]
    gate = _dot(x, wg_ref[...])
    up = _dot(x, wu_ref[...])
    o_ref[...] = ((gate * _sigmoid(gate)) * up).astype(o_ref.dtype)


def _swiglu_in(xn, w_in_bf, *, tm_pref, tf_pref):
    m, d = xn.shape
    f = w_in_bf.shape[1] // 2
    tm, tf = _tile(m, tm_pref), _tile(f, tf_pref)
    nf = f // tf
    return pl.pallas_call(
        _swiglu_in_body,
        grid=(m // tm, nf),
        in_specs=[
            pl.BlockSpec((tm, d), lambda i, j: (i, 0)),
            pl.BlockSpec((d, tf), lambda i, j: (0, j)),
            pl.BlockSpec((d, tf), lambda i, j: (0, j + nf)),
        ],
        out_specs=pl.BlockSpec((tm, tf), lambda i, j: (i, j)),
        out_shape=jax.ShapeDtypeStruct((m, f), BF16),
        compiler_params=_params("parallel", "parallel"),
        name="swiglu_in",
    )(xn, w_in_bf, w_in_bf)


def _gla_gate_body(x_ref, wg1_ref, wgate_ref, b_ref, o_ref):
    g1 = _dot(x_ref[...], wg1_ref[...])
    z = _dot(g1.astype(BF16), wgate_ref[...]) + b_ref[...]
    log_sig = jnp.minimum(z, 0.0) - jnp.log1p(jnp.exp(-jnp.abs(z)))
    o_ref[...] = log_sig / GLA_TAU


def _gla_gate(xn, wg1_bf, wgate_bf, b_gate):
    m, d = xn.shape
    dk = wgate_bf.shape[1]
    tm = _tile(m, 512)
    return pl.pallas_call(
        _gla_gate_body,
        grid=(m // tm,),
        in_specs=[
            pl.BlockSpec((tm, d), lambda i: (i, 0)),
            pl.BlockSpec((d, GLA_GATE_PAD), lambda i: (0, 0)),
            pl.BlockSpec((GLA_GATE_PAD, dk), lambda i: (0, 0)),
            pl.BlockSpec((1, dk), lambda i: (0, 0)),
        ],
        out_specs=pl.BlockSpec((tm, dk), lambda i: (i, 0)),
        out_shape=jax.ShapeDtypeStruct((m, dk), F32),
        compiler_params=_params("parallel"),
        name="gla_gate",
    )(xn, wg1_bf, wgate_bf, b_gate.reshape(1, dk))


def _gla_body(q_ref, k_ref, v_ref, r_ref, la_ref, s0_ref, gh_ref, y_ref, sout_ref, s_scr, *, chunk, n_chunks, q_scale):
    t = pl.program_id(2)

    @pl.when(t == 0)
    def _():
        s_scr[...] = s0_ref[0, 0]

    row = lax.broadcasted_iota(jnp.int32, (chunk, chunk), 0)
    col = lax.broadcasted_iota(jnp.int32, (chunk, chunk), 1)
    causal = row >= col
    tri = causal.astype(BF16)
    dk, dv = s_scr.shape
    gh = gh_ref[...]

    def step(c, carry):
        rows = pl.ds(pl.multiple_of(c * chunk, chunk), chunk)
        g_hi, g_mid, g_lo = _split3(la_ref[0, rows, :])
        b = _dot(tri, g_hi) + _dot(tri, g_mid) + _dot(tri, g_lo)
        b_last = b[chunk - 1:chunk, :]
        qf = q_ref[0, rows, :] * q_scale
        kf = k_ref[0, rows, :]
        vb = v_ref[0, rows, :].astype(BF16)
        q_dec = (qf * jnp.exp(b)).astype(BF16)
        att = _dot_nt(q_dec, (kf * jnp.exp(-b)).astype(BF16))
        att = jnp.where(causal, att, 0.0)
        s = s_scr[...]
        o = _dot(q_dec, s.astype(BF16)) + _dot(att.astype(BF16), vb)
        k_dec = (kf * jnp.exp(b_last - b)).astype(BF16)
        decay_col = jnp.transpose(jnp.broadcast_to(jnp.exp(b_last), (LANES_V7X, dk)))
        s_scr[...] = s * jnp.tile(decay_col, (1, dv // LANES_V7X)) + _dot_tn(k_dec, vb)
        ms = jnp.mean(o * o, axis=-1, keepdims=True)
        on = (o * lax.rsqrt(ms + RMS_EPS)) * gh
        rr = r_ref[0, rows, :]
        y_ref[0, rows, :] = (on * (rr * _sigmoid(rr))).astype(y_ref.dtype)
        return carry

    lax.fori_loop(0, n_chunks, step, 0)

    @pl.when(t == pl.num_programs(2) - 1)
    def _():
        sout_ref[0, 0] = s_scr[...]


def _gla_recurrence(proj, log_a, s0, g_head, *, heads, dk, dv):
    bsz, seq, _ = proj.shape
    dkh, dvh = dk // heads, dv // heads
    tt = _tile(seq, 256)
    assert tt % GLA_CHUNK == 0 and dvh % dkh == 0
    ratio = dvh // dkh
    body = functools.partial(_gla_body, chunk=GLA_CHUNK, n_chunks=tt // GLA_CHUNK, q_scale=dkh ** -0.5)
    return pl.pallas_call(
        body,
        grid=(bsz, heads, seq // tt),
        in_specs=[
            pl.BlockSpec((1, tt, dkh), lambda b, h, t: (b, t, h)),
            pl.BlockSpec((1, tt, dkh), lambda b, h, t: (b, t, heads + h)),
            pl.BlockSpec((1, tt, dvh), lambda b, h, t: (b, t, 2 * heads // ratio + h)),
            pl.BlockSpec((1, tt, dvh), lambda b, h, t: (b, t, 2 * heads // ratio + heads + h)),
            pl.BlockSpec((1, tt, dkh), lambda b, h, t: (b, t, h)),
            pl.BlockSpec((1, 1, dkh, dvh), lambda b, h, t: (b, h, 0, 0)),
            pl.BlockSpec((1, dvh), lambda b, h, t: (0, 0)),
        ],
        out_specs=[
            pl.BlockSpec((1, tt, dvh), lambda b, h, t: (b, t, h)),
            pl.BlockSpec((1, 1, dkh, dvh), lambda b, h, t: (b, h, 0, 0)),
        ],
        out_shape=[
            jax.ShapeDtypeStruct((bsz, seq, dv), BF16),
            jax.ShapeDtypeStruct((bsz, heads, dkh, dvh), F32),
        ],
        scratch_shapes=[pltpu.VMEM((dkh, dvh), F32)],
        compiler_params=_params("parallel", "parallel", "arbitrary"),
        name="gla_recurrence",
    )(proj, proj, proj, proj, log_a, s0, g_head.reshape(1, dvh))


def _gla_mixer(xn, h, s0, seq, w_in_bf, wg1_bf, wgate_bf, b_gate, g_head, w_out_bf, *, heads, dk, dv):
    m = xn.shape[0]
    bsz = m // seq
    proj = _matmul(xn, w_in_bf, n_out=2 * dk + 2 * dv, out_dtype=F32, name="gla_in")
    log_a = _gla_gate(xn, wg1_bf, wgate_bf, b_gate)
    proj = proj.reshape(bsz, seq, -1)
    log_a = log_a.reshape(bsz, seq, dk)
    pad = (-seq) % GLA_CHUNK
    if pad:
        proj = jnp.pad(proj, ((0, 0), (0, pad), (0, 0)))
        log_a = jnp.pad(log_a, ((0, 0), (0, pad), (0, 0)))
    y, s_new = _gla_recurrence(proj, log_a, s0, g_head, heads=heads, dk=dk, dv=dv)
    y = y[:, :seq].reshape(m, dv)
    return _matmul(y, w_out_bf, n_out=w_out_bf.shape[1], res=h, name="gla_out"), s_new


def _sb_prompt_body(bias_ref, q_ref, k_ref, v_ref, o_ref, kb_scr, vb_scr, *, tq, scale):
    h = pl.program_id(1)
    i = pl.program_id(2)
    tk = SB_KEY_BLOCK

    @pl.when(i == 0)
    def _():
        kb_scr[...] = k_ref[0].astype(BF16)
        vb_scr[...] = v_ref[0].astype(BF16)

    qb = q_ref[0]
    bias = bias_ref[h]
    q0 = i * tq
    n_kb = (q0 + tq) // tk
    upper = (lax.broadcasted_iota(jnp.int32, (tk, tk), 0) > lax.broadcasted_iota(jnp.int32, (tk, tk), 1)).astype(BF16)
    q_pos = q0 + lax.broadcasted_iota(jnp.int32, (tq, tk), 0)
    k_off = lax.broadcasted_iota(jnp.int32, (tq, tk), 1)

    def step(jj, carry):
        o, tail = carry
        k0 = pl.multiple_of((n_kb - 1 - jj) * tk, tk)
        z = _dot_nt(qb, kb_scr[pl.ds(k0, tk), :]) * scale + bias
        sp = _softplus(z)
        vis = (k0 + k_off) < q_pos
        lk = jnp.where(vis, -sp, 0.0)
        lk_hi, lk_mid, lk_lo = _split3(lk)
        later = _dot(lk_hi, upper) + _dot(lk_mid, upper) + _dot(lk_lo, upper)
        a = jnp.where(vis, jnp.exp((tail + later) + (z - sp)), 0.0)
        o = o + _dot(a.astype(BF16), vb_scr[pl.ds(k0, tk), :])
        tail = tail + jnp.sum(lk, axis=-1, keepdims=True)
        return o, tail

    o, _ = lax.fori_loop(0, n_kb, step, (jnp.zeros((tq, qb.shape[1]), F32), jnp.zeros((tq, 1), F32)))
    o_ref[0] = o.astype(o_ref.dtype)


def _sb_prompt(q, k, v, bias, *, heads):
    bsz, seq, d = q.shape
    hd = d // heads
    assert hd == LANES_V7X and seq % SB_KEY_BLOCK == 0
    tq = _tile(seq, 256)
    body = functools.partial(_sb_prompt_body, tq=tq, scale=hd ** -0.5)
    return pl.pallas_call(
        body,
        grid=(bsz, heads, seq // tq),
        in_specs=[
            pl.BlockSpec(memory_space=pltpu.SMEM),
            pl.BlockSpec((1, tq, hd), lambda b, h, i: (b, i, h)),
            pl.BlockSpec((1, seq, hd), lambda b, h, i: (b, 0, h)),
            pl.BlockSpec((1, seq, hd), lambda b, h, i: (b, 0, h)),
        ],
        out_specs=pl.BlockSpec((1, tq, hd), lambda b, h, i: (b, i, h)),
        out_shape=jax.ShapeDtypeStruct((bsz, seq, d), BF16),
        scratch_shapes=[pltpu.VMEM((seq, hd), BF16), pltpu.VMEM((seq, hd), BF16)],
        compiler_params=_params("parallel", "parallel", "arbitrary"),
        name="sb_prompt",
    )(bias, q, k, v)


def _sb_sample_body(pt_ref, qbd_ref, bias_ref, kn_ref, vn_ref, kp_ref, vp_ref, o_ref, acc_scr, tail_scr,
                    *, heads, n_new, scale):
    del pt_ref
    p = pl.program_id(1)
    page, nl = kn_ref.shape[1], qbd_ref.shape[2]
    lower = (lax.broadcasted_iota(jnp.int32, (page, page), 1) > lax.broadcasted_iota(jnp.int32, (page, page), 0)).astype(BF16)

    def block(k_ref, v_ref, vis):
        z = _dot(k_ref[0].astype(BF16), qbd_ref[0]) * scale + bias_ref[...]
        sp = _softplus(z)
        lk = -sp if vis is None else jnp.where(vis, -sp, 0.0)
        lk_hi, lk_mid, lk_lo = _split3(lk)
        later = _dot(lower, lk_hi) + _dot(lower, lk_mid) + _dot(lower, lk_lo)
        a = jnp.exp((tail_scr[0:1, :] + later) + (z - sp))
        if vis is not None:
            a = jnp.where(vis, a, 0.0)
        acc_scr[...] += _dot(jnp.transpose(a).astype(BF16), v_ref[0].astype(BF16))
        tail_scr[0:1, :] = tail_scr[0:1, :] + jnp.sum(lk, axis=0, keepdims=True)

    @pl.when(p == 0)
    def _():
        acc_scr[...] = jnp.zeros_like(acc_scr)
        tail_scr[...] = jnp.zeros_like(tail_scr)
        key = lax.broadcasted_iota(jnp.int32, (page, nl), 0)
        tok = lax.broadcasted_iota(jnp.int32, (page, nl), 1) // heads
        block(kn_ref, vn_ref, (key < tok) & (key < n_new))

    @pl.when(p > 0)
    def _():
        block(kp_ref, vp_ref, None)

    @pl.when(p == pl.num_programs(1) - 1)
    def _():
        d = acc_scr.shape[1]
        hd = d // heads
        own = lax.broadcasted_iota(jnp.int32, (heads, d), 0) == lax.broadcasted_iota(jnp.int32, (heads, d), 1) // hd
        for t in range(n_new):
            rows = acc_scr[t * heads:(t + 1) * heads, :]
            o_ref[0, t:t + 1, :] = jnp.sum(jnp.where(own, rows, 0.0), axis=0, keepdims=True).astype(o_ref.dtype)


def _sb_sample(q, k_new, v_new, cache_k, cache_v, page_table, bias, *, heads):
    bd, t_new, d = q.shape
    hd = d // heads
    page = cache_k.shape[1]
    n_pages = page_table.shape[1]
    nl = LANES_V7X
    assert heads * t_new <= nl and heads % 8 == 0 and hd % LANES_V7X == 0 and t_new <= page
    q4 = q.reshape(bd, t_new, heads, hd)
    qbd = jnp.einsum("bthx,hg->bhxtg", q4, jnp.eye(heads, dtype=q.dtype)).reshape(bd, d, t_new * heads)
    qbd = jnp.pad(qbd, ((0, 0), (0, 0), (0, nl - t_new * heads))).astype(BF16)
    bias_l = jnp.pad(jnp.tile(bias, t_new), (0, nl - t_new * heads)).reshape(1, nl)
    kn = jnp.pad(k_new, ((0, 0), (0, page - t_new), (0, 0)))
    vn = jnp.pad(v_new, ((0, 0), (0, page - t_new), (0, 0)))

    def page_map(b, p, pt):
        return (pt[b, n_pages - jnp.maximum(p, 1)], 0, 0)

    body = functools.partial(_sb_sample_body, heads=heads, n_new=t_new, scale=hd ** -0.5)
    grid_spec = pltpu.PrefetchScalarGridSpec(
        num_scalar_prefetch=1,
        grid=(bd, n_pages + 1),
        in_specs=[
            pl.BlockSpec((1, d, nl), lambda b, p, pt: (b, 0, 0)),
            pl.BlockSpec((1, nl), lambda b, p, pt: (0, 0)),
            pl.BlockSpec((1, page, d), lambda b, p, pt: (b, 0, 0)),
            pl.BlockSpec((1, page, d), lambda b, p, pt: (b, 0, 0)),
            pl.BlockSpec((1, page, d), page_map),
            pl.BlockSpec((1, page, d), page_map),
        ],
        out_specs=pl.BlockSpec((1, t_new, d), lambda b, p, pt: (b, 0, 0)),
        scratch_shapes=[pltpu.VMEM((nl, d), F32), pltpu.VMEM((8, nl), F32)],
    )
    return pl.pallas_call(
        body,
        grid_spec=grid_spec,
        out_shape=jax.ShapeDtypeStruct((bd, t_new, d), F32),
        compiler_params=_params("parallel", "arbitrary"),
        name="sb_sample",
    )(page_table, qbd, bias_l, kn, vn, cache_k, cache_v)


def _conv_prompt_body(x_ref, wb_ref, wc_ref, wh_ref, wconv_ref, buf0_ref, y_ref, st_ref, u_scr, *, blocks_per_seq):
    m = pl.program_id(1)
    tm = x_ref.shape[0]
    x = x_ref[...]
    bg = _dot(x, wb_ref[...])
    u = _dot(x, wc_ref[...]) * _dot(x, wh_ref[...])

    @pl.when(m % blocks_per_seq == 0)
    def _():
        u_scr[6:8, :] = buf0_ref[0]

    u_scr[8:8 + tm, :] = u
    conv = wconv_ref[0:1, :] * u_scr[6:6 + tm, :] + wconv_ref[1:2, :] * u_scr[7:7 + tm, :] + wconv_ref[2:3, :] * u
    y_ref[...] = (bg * conv).astype(y_ref.dtype)
    st_ref[0] = u_scr[tm + 6:tm + 8, :]
    u_scr[0:8, :] = u_scr[tm:tm + 8, :]


def _conv_prompt(xn, w_in_bf, w_conv, buf0, seq):
    m, d = xn.shape
    tm = _tile(seq, 1024)
    tn = _tile(d, 256)
    nb = d // tn
    blocks_per_seq = seq // tm
    body = functools.partial(_conv_prompt_body, blocks_per_seq=blocks_per_seq)
    return pl.pallas_call(
        body,
        grid=(nb, m // tm),
        in_specs=[
            pl.BlockSpec((tm, d), lambda n, i: (i, 0)),
            pl.BlockSpec((d, tn), lambda n, i: (0, n)),
            pl.BlockSpec((d, tn), lambda n, i: (0, nb + n)),
            pl.BlockSpec((d, tn), lambda n, i: (0, 2 * nb + n)),
            pl.BlockSpec((w_conv.shape[0], tn), lambda n, i: (0, n)),
            pl.BlockSpec((1, 2, tn), lambda n, i: (i // blocks_per_seq, 0, n)),
        ],
        out_specs=[
            pl.BlockSpec((tm, tn), lambda n, i: (i, n)),
            pl.BlockSpec((1, 2, tn), lambda n, i: (i // blocks_per_seq, 0, n)),
        ],
        out_shape=[
            jax.ShapeDtypeStruct((m, d), BF16),
            jax.ShapeDtypeStruct((m // seq, 2, d), F32),
        ],
        scratch_shapes=[pltpu.VMEM((tm + 8, tn), F32)],
        compiler_params=_params("parallel", "arbitrary"),
        name="conv_prompt",
    )(xn, w_in_bf, w_in_bf, w_in_bf, w_conv, buf0)


def _conv_sample_body(proj_ref, buf0_ref, wconv_ref, y_ref, st_ref, *, seq):
    d = y_ref.shape[2]
    taps = [buf0_ref[0], buf0_ref[1]]
    for t in range(seq):
        taps.append(proj_ref[t, :, d:2 * d] * proj_ref[t, :, 2 * d:3 * d])
    for t in range(seq):
        conv = wconv_ref[0:1, :] * taps[t] + wconv_ref[1:2, :] * taps[t + 1] + wconv_ref[2:3, :] * taps[t + 2]
        y_ref[t] = (proj_ref[t, :, 0:d] * conv).astype(y_ref.dtype)
    st_ref[0] = taps[seq]
    st_ref[1] = taps[seq + 1]


def _conv_sample(proj_tm, buf0_tm, w_conv):
    seq, bd, d3 = proj_tm.shape
    d = d3 // 3
    return pl.pallas_call(
        functools.partial(_conv_sample_body, seq=seq),
        out_shape=[jax.ShapeDtypeStruct((seq, bd, d), BF16), jax.ShapeDtypeStruct((2, bd, d), F32)],
        compiler_params=pltpu.CompilerParams(vmem_limit_bytes=VMEM_LIMIT_V7X),
        name="conv_sample",
    )(proj_tm, buf0_tm, w_conv)


def kernel(x_prompt, x_sample, state_gla, cache_sb_k, cache_sb_v, state_conv, cache_mem_k, cache_mem_v, page_table, mem_prompt, g_mix, g_xattn, g_mem, g_ffn, g_final, w_gla_in, w_gla_gate, b_gla_gate, g_gla_head, w_gla_out, w_sb_in, b_sb, w_sb_out, w_conv_in, w_conv, w_conv_out, w_xq, w_xkv, w_xo, w_ffn_in, w_ffn_out):
    bsz, seq, d = x_prompt.shape
    bd, t_new, _ = x_sample.shape
    depth = g_mix.shape[0]
    n_mem = mem_prompt.shape[1]
    gla_heads = state_gla.shape[2]
    dk = w_gla_gate.shape[2]
    dv = w_gla_out.shape[1]
    sb_heads = b_sb.shape[1]
    x_heads = cache_mem_k.shape[3]
    xw = w_xq.shape[2]
    page = cache_sb_k.shape[2]
    mp, ms = bsz * seq, bd * t_new

    bf = lambda w: w.astype(BF16)
    w_gla_main = bf(w_gla_in[:, :, :2 * dk + 2 * dv])
    rank = w_gla_gate.shape[1]
    w_gla_g1 = bf(jnp.pad(w_gla_in[:, :, 2 * dk + 2 * dv:], ((0, 0), (0, 0), (0, GLA_GATE_PAD - rank))))
    w_gla_gate_p = bf(jnp.pad(w_gla_gate, ((0, 0), (0, GLA_GATE_PAD - rank), (0, 0))))
    w_gla_out_bf, w_sb_in_bf, w_sb_out_bf = bf(w_gla_out), bf(w_sb_in), bf(w_sb_out)
    w_conv_in_bf, w_conv_out_bf = bf(w_conv_in), bf(w_conv_out)
    w_xq_bf, w_xkv_bf, w_xo_bf = bf(w_xq), bf(w_xkv), bf(w_xo)
    w_ffn_in_bf, w_ffn_out_bf = bf(w_ffn_in), bf(w_ffn_out)

    hp = x_prompt.reshape(mp, d)
    hs = x_sample.reshape(ms, d)
    mem_k, mem_v = _mem_kv(mem_prompt.reshape(bsz * n_mem, d), g_mem, w_xkv_bf)

    gla_p, gla_s, sbk_p, sbv_p, sbk_s, sbv_s, conv_p, conv_s = [], [], [], [], [], [], [], []
    for i in range(depth):
        kind, j = i % 3, i // 3
        xp = _rmsnorm(hp, g_mix[i], BF16)
        xs = _rmsnorm(hs, g_mix[i], BF16)
        if kind == 0:
            gla = functools.partial(_gla_mixer, w_in_bf=w_gla_main[j], wg1_bf=w_gla_g1[j], wgate_bf=w_gla_gate_p[j],
                                    b_gate=b_gla_gate[j], g_head=g_gla_head[j], w_out_bf=w_gla_out_bf[j],
                                    heads=gla_heads, dk=dk, dv=dv)
            zero_state = jnp.zeros((bsz,) + state_gla.shape[2:], F32)
            hp, sp = gla(xp, hp, zero_state, seq)
            hs, ss = gla(xs, hs, state_gla[j], t_new)
            gla_p.append(sp)
            gla_s.append(ss)
        elif kind == 1:
            w_in = w_sb_in_bf[j]
            qp = _matmul(xp, w_in, n_out=d, col_off=0, out_dtype=BF16, name="sb_q")
            kp = _matmul(xp, w_in, n_out=d, col_off=d, name="sb_k")
            vp = _matmul(xp, w_in, n_out=d, col_off=2 * d, name="sb_v")
            op = _sb_prompt(qp.reshape(bsz, seq, d), kp.reshape(bsz, seq, d), vp.reshape(bsz, seq, d), b_sb[j],
                            heads=sb_heads)
            hp = _matmul(op.reshape(mp, d), w_sb_out_bf[j], n_out=d, res=hp, name="sb_out")
            qkv = _matmul(xs, w_in, n_out=3 * d, name="sb_qkv_sample")
            qs, kn, vn = (qkv[:, c * d:(c + 1) * d].reshape(bd, t_new, d) for c in range(3))
            os_ = _sb_sample(qs, kn, vn, cache_sb_k[j].reshape(-1, page, d), cache_sb_v[j].reshape(-1, page, d),
                             page_table, b_sb[j], heads=sb_heads)
            hs = _matmul(os_.reshape(ms, d).astype(BF16), w_sb_out_bf[j], n_out=d, res=hs, name="sb_out")
            sbk_p.append(kp)
            sbv_p.append(vp)
            sbk_s.append(kn)
            sbv_s.append(vn)
        else:
            yp, bp = _conv_prompt(xp, w_conv_in_bf[j], w_conv[j], jnp.zeros((bsz, 2, d), F32), seq)
            hp = _matmul(yp, w_conv_out_bf[j], n_out=d, res=hp, name="conv_out")
            proj_s = _matmul(xs, w_conv_in_bf[j], n_out=3 * d, name="conv_in_sample")
            ys, bs = _conv_sample(proj_s.reshape(bd, t_new, 3 * d).transpose(1, 0, 2),
                                  state_conv[j].transpose(1, 0, 2), w_conv[j])
            hs = _matmul(ys.transpose(1, 0, 2).reshape(ms, d), w_conv_out_bf[j], n_out=d, res=hs, name="conv_out")
            conv_p.append(bp)
            conv_s.append(bs.transpose(1, 0, 2))
        hp = _xattn(hp, g_xattn[i], w_xq_bf[i], mem_k[i], mem_v[i], w_xo_bf[i],
                    heads=x_heads, rows_per_seq=seq, n_mem=n_mem, grouped=False)
        hs = _xattn(hs, g_xattn[i], w_xq_bf[i], cache_mem_k[i].reshape(bd * n_mem, xw),
                    cache_mem_v[i].reshape(bd * n_mem, xw), w_xo_bf[i],
                    heads=x_heads, rows_per_seq=t_new, n_mem=n_mem, grouped=True)
        ap = _swiglu_in(_rmsnorm(hp, g_ffn[i], BF16), w_ffn_in_bf[i], tm_pref=2048, tf_pref=256)
        hp = _matmul(ap, w_ffn_out_bf[i], n_out=d, res=hp, tm_pref=512, tn_pref=256, name="ffn_out")
        as_ = _swiglu_in(_rmsnorm(hs, g_ffn[i], BF16), w_ffn_in_bf[i], tm_pref=2048, tf_pref=256)
        hs = _matmul(as_, w_ffn_out_bf[i], n_out=d, res=hs, tm_pref=512, tn_pref=256, name="ffn_out")

    y_prompt = _rmsnorm(hp, g_final, F32).reshape(bsz, seq, d)
    y_sample = _rmsnorm(hs, g_final, F32).reshape(bd, t_new, d)
    hd = d // sb_heads
    sb_p = lambda xs_: jnp.stack(xs_).reshape(-1, bsz, seq, sb_heads, hd)
    sb_s = lambda xs_: jnp.stack(xs_).reshape(-1, bd, t_new, sb_heads, hd)
    xhd = xw // x_heads
    return (y_prompt, y_sample,
            jnp.stack(gla_p), jnp.stack(gla_s),
            sb_p(sbk_p), sb_p(sbv_p), sb_s(sbk_s), sb_s(sbv_s),
            jnp.stack(conv_p), jnp.stack(conv_s),
            mem_k.reshape(depth, bsz, n_mem, x_heads, xhd), mem_v.reshape(depth, bsz, n_mem, x_heads, xhd))
```

```python
import functools

import jax
import jax.numpy as jnp
from jax import lax
from jax.experimental import pallas as pl
from jax.experimental.pallas import tpu as pltpu

F32 = jnp.float32
BF16 = jnp.bfloat16

LANES_V7X = 128
VMEM_LIMIT_V7X = 56 * 1024 * 1024
RMS_EPS = 1e-6
GLA_TAU = 16.0
GLA_CHUNK = 32
GLA_GATE_PAD = LANES_V7X
GLA_HEAD_GROUP = 2
SB_BLOCK = 256
SB_ROWS = 128
NEG_BIG = -1e30


def _params(*sem):
    return pltpu.CompilerParams(dimension_semantics=sem, vmem_limit_bytes=VMEM_LIMIT_V7X)


def _tile(n, pref):
    if n <= pref:
        return n
    t = (pref // LANES_V7X) * LANES_V7X
    while t > LANES_V7X and n % t:
        t -= LANES_V7X
    assert n % t == 0, (n, pref)
    return t


def _softplus(z):
    return jnp.maximum(z, 0.0) + jnp.log(1.0 + jnp.exp(-jnp.abs(z)))


def _sigmoid(z):
    return 1.0 / (1.0 + jnp.exp(-z))


def _split2(x):
    hi = x.astype(BF16)
    lo = (x - hi.astype(F32)).astype(BF16)
    return hi, lo


def _split3(x):
    hi = x.astype(BF16)
    r1 = x - hi.astype(F32)
    mid = r1.astype(BF16)
    lo = (r1 - mid.astype(F32)).astype(BF16)
    return hi, mid, lo


def _dot(a, b):
    return jnp.dot(a, b, preferred_element_type=F32)


def _dot_nt(a, b):
    return lax.dot_general(a, b, (((1,), (1,)), ((), ())), preferred_element_type=F32)


def _dot_tn(a, b):
    return lax.dot_general(a, b, (((0,), (0,)), ((), ())), preferred_element_type=F32)


def _rmsnorm_body(x_ref, g_ref, o_ref):
    x = x_ref[...]
    ms = jnp.mean(x * x, axis=-1, keepdims=True)
    o_ref[...] = ((x * lax.rsqrt(ms + RMS_EPS)) * g_ref[...]).astype(o_ref.dtype)


def _rmsnorm(x, g, out_dtype):
    m, d = x.shape
    tm = _tile(m, 512)
    return pl.pallas_call(
        _rmsnorm_body,
        grid=(m // tm,),
        in_specs=[pl.BlockSpec((tm, d), lambda i: (i, 0)), pl.BlockSpec((1, d), lambda i: (0, 0))],
        out_specs=pl.BlockSpec((tm, d), lambda i: (i, 0)),
        out_shape=jax.ShapeDtypeStruct((m, d), out_dtype),
        compiler_params=_params("parallel"),
        name="rmsnorm",
    )(x, g.reshape(1, d))


def _mm_body(x_ref, w_ref, o_ref):
    o_ref[...] = _dot(x_ref[...], w_ref[0].astype(BF16)).astype(o_ref.dtype)


def _mm_res_body(x_ref, w_ref, r_ref, o_ref):
    o_ref[...] = (_dot(x_ref[...], w_ref[0].astype(BF16)) + r_ref[...]).astype(o_ref.dtype)


def _matmul(x, w, layer, *, n_out, col_off=0, res=None, out_dtype=F32, tm_pref=1024, tn_pref=512, name="matmul"):
    m, k = x.shape
    tm, tn = _tile(m, tm_pref), _tile(n_out, tn_pref)
    assert col_off % tn == 0
    off = col_off // tn
    in_specs = [pl.BlockSpec((tm, k), lambda i, j: (i, 0)),
                pl.BlockSpec((1, k, tn), lambda i, j: (layer, 0, j + off))]
    args = [x, w]
    body = _mm_body
    if res is not None:
        in_specs.append(pl.BlockSpec((tm, tn), lambda i, j: (i, j)))
        args.append(res)
        body = _mm_res_body
    return pl.pallas_call(
        body,
        grid=(m // tm, n_out // tn),
        in_specs=in_specs,
        out_specs=pl.BlockSpec((tm, tn), lambda i, j: (i, j)),
        out_shape=jax.ShapeDtypeStruct((m, n_out), out_dtype),
        compiler_params=_params("parallel", "parallel"),
        name=name,
    )(*args)


def _mem_kv_body(x_ref, g_ref, w_ref, k_ref, v_ref):
    x = x_ref[...]
    ms = jnp.mean(x * x, axis=-1, keepdims=True)
    xn = ((x * lax.rsqrt(ms + RMS_EPS)) * g_ref[0]).astype(BF16)
    kv = _dot(xn, w_ref[0])
    half = kv.shape[1] // 2
    k_ref[0] = kv[:, :half]
    v_ref[0] = kv[:, half:]


def _mem_kv(mem, g_mem, w_xkv_bf):
    rows, d = mem.shape
    depth, _, two_w = w_xkv_bf.shape
    xw = two_w // 2
    tm = _tile(rows, 512)
    out = jax.ShapeDtypeStruct((depth, rows, xw), F32)
    return pl.pallas_call(
        _mem_kv_body,
        grid=(depth, rows // tm),
        in_specs=[
            pl.BlockSpec((tm, d), lambda i, m: (m, 0)),
            pl.BlockSpec((1, 1, d), lambda i, m: (i, 0, 0)),
            pl.BlockSpec((1, d, two_w), lambda i, m: (i, 0, 0)),
        ],
        out_specs=[pl.BlockSpec((1, tm, xw), lambda i, m: (i, m, 0))] * 2,
        out_shape=[out, out],
        compiler_params=_params("parallel", "parallel"),
        name="mem_kv",
    )(mem, g_mem.reshape(depth, 1, d), w_xkv_bf)


def _xattn_body(x_ref, g_ref, wq_ref, mk_ref, mv_ref, wo_ref, o_ref, *, heads, scale, rows_per_group, mem_per_group):
    x = x_ref[...]
    ms = jnp.mean(x * x, axis=-1, keepdims=True)
    xn = ((x * lax.rsqrt(ms + RMS_EPS)) * g_ref[0]).astype(BF16)
    q = _dot(xn, wq_ref[0])
    mk = mk_ref[0].astype(BF16)
    mv = mv_ref[0].astype(BF16)
    hd = q.shape[1] // heads
    outs = []
    for h in range(heads):
        sl = slice(h * hd, (h + 1) * hd)
        s = _dot_nt(q[:, sl].astype(BF16), mk[:, sl]) * scale
        if rows_per_group is not None:
            rg = lax.broadcasted_iota(jnp.int32, s.shape, 0) // rows_per_group
            cg = lax.broadcasted_iota(jnp.int32, s.shape, 1) // mem_per_group
            s = jnp.where(rg == cg, s, NEG_BIG)
        e = jnp.exp(s - jnp.max(s, axis=-1, keepdims=True))
        p = e / jnp.sum(e, axis=-1, keepdims=True)
        outs.append(_dot(p.astype(BF16), mv[:, sl]))
    o = jnp.concatenate(outs, axis=-1).astype(BF16)
    o_ref[...] = _dot(o, wo_ref[0]) + x


def _xattn(h, g, wq_bf, mk, mv, wo_bf, layer, *, heads, rows_per_seq, n_mem, grouped):
    m, d = h.shape
    xw = wq_bf.shape[2]
    scale = (xw // heads) ** -0.5
    if grouped:
        tm, mem_rows = m, mk.shape[1]
        mem_map = lambda i: (layer, 0, 0)
        rpg = rows_per_seq
    else:
        tm, mem_rows = _tile(rows_per_seq, 256), n_mem
        blocks_per_seq = rows_per_seq // tm
        mem_map = lambda i: (layer, i // blocks_per_seq, 0)
        rpg = None
    body = functools.partial(_xattn_body, heads=heads, scale=scale, rows_per_group=rpg, mem_per_group=n_mem)
    return pl.pallas_call(
        body,
        grid=(m // tm,),
        in_specs=[
            pl.BlockSpec((tm, d), lambda i: (i, 0)),
            pl.BlockSpec((1, 1, d), lambda i: (layer, 0, 0)),
            pl.BlockSpec((1, d, xw), lambda i: (layer, 0, 0)),
            pl.BlockSpec((1, mem_rows, xw), mem_map),
            pl.BlockSpec((1, mem_rows, xw), mem_map),
            pl.BlockSpec((1, xw, d), lambda i: (layer, 0, 0)),
        ],
        out_specs=pl.BlockSpec((tm, d), lambda i: (i, 0)),
        out_shape=jax.ShapeDtypeStruct((m, d), F32),
        compiler_params=_params("parallel"),
        name="xattn",
    )(h, g, wq_bf, mk, mv, wo_bf)


def _swiglu_in_body(x_ref, wg_ref, wu_ref, o_ref):
    x = x_ref[...]
    gate = _dot(x, wg_ref[0].astype(BF16))
    up = _dot(x, wu_ref[0].astype(BF16))
    o_ref[...] = ((gate * _sigmoid(gate)) * up).astype(o_ref.dtype)


def _swiglu_in(xn, w_in, layer, *, tm_pref, tf_pref):
    m, d = xn.shape
    f = w_in.shape[2] // 2
    tm, tf = _tile(m, tm_pref), _tile(f, tf_pref)
    nf = f // tf
    return pl.pallas_call(
        _swiglu_in_body,
        grid=(m // tm, nf),
        in_specs=[
            pl.BlockSpec((tm, d), lambda i, j: (i, 0), pipeline_mode=pl.Buffered(1)),
            pl.BlockSpec((1, d, tf), lambda i, j: (layer, 0, j)),
            pl.BlockSpec((1, d, tf), lambda i, j: (layer, 0, j + nf)),
        ],
        out_specs=pl.BlockSpec((tm, tf), lambda i, j: (i, j)),
        out_shape=jax.ShapeDtypeStruct((m, f), BF16),
        compiler_params=_params("parallel", "parallel"),
        name="swiglu_in",
    )(xn, w_in, w_in)


def _gla_gate_body(x_ref, wg1_ref, wgate_ref, b_ref, o_ref):
    g1 = _dot(x_ref[...], wg1_ref[0])
    z = _dot(g1.astype(BF16), wgate_ref[0]) + b_ref[0]
    log_sig = jnp.minimum(z, 0.0) - jnp.log1p(jnp.exp(-jnp.abs(z)))
    o_ref[...] = log_sig / GLA_TAU


def _gla_gate(xn, wg1_bf, wgate_bf, b_gate, layer):
    m, d = xn.shape
    dk = wgate_bf.shape[2]
    tm = _tile(m, 512)
    return pl.pallas_call(
        _gla_gate_body,
        grid=(m // tm,),
        in_specs=[
            pl.BlockSpec((tm, d), lambda i: (i, 0)),
            pl.BlockSpec((1, d, GLA_GATE_PAD), lambda i: (layer, 0, 0)),
            pl.BlockSpec((1, GLA_GATE_PAD, dk), lambda i: (layer, 0, 0)),
            pl.BlockSpec((1, 1, dk), lambda i: (layer, 0, 0)),
        ],
        out_specs=pl.BlockSpec((tm, dk), lambda i: (i, 0)),
        out_shape=jax.ShapeDtypeStruct((m, dk), F32),
        compiler_params=_params("parallel"),
        name="gla_gate",
    )(xn, wg1_bf, wgate_bf, b_gate)


def _gla_body(q_ref, k_ref, v_ref, r_ref, la_ref, s0_ref, gh_ref, y_ref, sout_ref, s_scr, *, chunk, n_chunks, q_scale):
    t = pl.program_id(2)
    group, dkh, dvh = s_scr.shape

    @pl.when(t == 0)
    def _():
        for g in range(group):
            s_scr[g] = s0_ref[0, 0, g]

    row = lax.broadcasted_iota(jnp.int32, (chunk, chunk), 0)
    col = lax.broadcasted_iota(jnp.int32, (chunk, chunk), 1)
    causal = row >= col
    tri = causal.astype(BF16)
    gh = gh_ref[0]
    ksl = [slice(g * dkh, (g + 1) * dkh) for g in range(group)]
    vsl = [slice(g * dvh, (g + 1) * dvh) for g in range(group)]

    def step(c, carry):
        rows = pl.ds(pl.multiple_of(c * chunk, chunk), chunk)
        g_hi, g_mid, g_lo = _split3(la_ref[0, rows, :])
        b = _dot(tri, g_hi) + _dot(tri, g_mid) + _dot(tri, g_lo)
        b_last = b[chunk - 1:chunk, :]
        qf = q_ref[0, rows, :] * q_scale
        kf = k_ref[0, rows, :]
        vb = v_ref[0, rows, :].astype(BF16)
        q_dec = (qf * jnp.exp(b)).astype(BF16)
        k_inv = (kf * jnp.exp(-b)).astype(BF16)
        k_dec = (kf * jnp.exp(b_last - b)).astype(BF16)
        decay = jnp.exp(b_last)
        att = [_dot_nt(q_dec[:, ks], k_inv[:, ks]) for ks in ksl]
        att = [jnp.where(causal, a, 0.0).astype(BF16) for a in att]
        s = [s_scr[g] for g in range(group)]
        o = [_dot(q_dec[:, ks], s_.astype(BF16)) + _dot(a, vb[:, vs]) for ks, vs, s_, a in zip(ksl, vsl, s, att)]
        upd = [_dot_tn(k_dec[:, ks], vb[:, vs]) for ks, vs in zip(ksl, vsl)]
        for g in range(group):
            decay_col = jnp.transpose(jnp.broadcast_to(decay[:, ksl[g]], (LANES_V7X, dkh)))
            s_scr[g] = s[g] * jnp.tile(decay_col, (1, dvh // LANES_V7X)) + upd[g]
        for g in range(group):
            ms = jnp.mean(o[g] * o[g], axis=-1, keepdims=True)
            on = (o[g] * lax.rsqrt(ms + RMS_EPS)) * gh
            rr = r_ref[0, rows, vsl[g]]
            y_ref[0, rows, vsl[g]] = (on * (rr * _sigmoid(rr))).astype(y_ref.dtype)
        return carry

    lax.fori_loop(0, n_chunks, step, 0)

    @pl.when(t == pl.num_programs(2) - 1)
    def _():
        for g in range(group):
            sout_ref[0, g] = s_scr[g]


def _gla_recurrence(proj, log_a, s0, s0_layer, g_head, layer, *, heads, dk, dv):
    bsz, seq, _ = proj.shape
    dkh, dvh = dk // heads, dv // heads
    group = GLA_HEAD_GROUP
    tt = _tile(seq, 256)
    assert tt % GLA_CHUNK == 0 and heads % group == 0 and (2 * dk) % (group * dvh) == 0
    hg = heads // group
    v_off = 2 * dk // (group * dvh)
    body = functools.partial(_gla_body, chunk=GLA_CHUNK, n_chunks=tt // GLA_CHUNK, q_scale=dkh ** -0.5)
    return pl.pallas_call(
        body,
        grid=(bsz, hg, seq // tt),
        in_specs=[
            pl.BlockSpec((1, tt, group * dkh), lambda b, h, t: (b, t, h)),
            pl.BlockSpec((1, tt, group * dkh), lambda b, h, t: (b, t, hg + h)),
            pl.BlockSpec((1, tt, group * dvh), lambda b, h, t: (b, t, v_off + h)),
            pl.BlockSpec((1, tt, group * dvh), lambda b, h, t: (b, t, v_off + hg + h)),
            pl.BlockSpec((1, tt, group * dkh), lambda b, h, t: (b, t, h)),
            pl.BlockSpec((1, 1, group, dkh, dvh), lambda b, h, t: (s0_layer, b, h, 0, 0)),
            pl.BlockSpec((1, 1, dvh), lambda b, h, t: (layer, 0, 0)),
        ],
        out_specs=[
            pl.BlockSpec((1, tt, group * dvh), lambda b, h, t: (b, t, h)),
            pl.BlockSpec((1, group, dkh, dvh), lambda b, h, t: (b, h, 0, 0)),
        ],
        out_shape=[
            jax.ShapeDtypeStruct((bsz, seq, dv), BF16),
            jax.ShapeDtypeStruct((bsz, heads, dkh, dvh), F32),
        ],
        scratch_shapes=[pltpu.VMEM((group, dkh, dvh), F32)],
        compiler_params=_params("parallel", "parallel", "arbitrary"),
        name="gla_recurrence",
    )(proj, proj, proj, proj, log_a, s0, g_head)


def _gla_mixer(xn, h, s0, s0_layer, seq, layer, w_in, wg1_bf, wgate_bf, b_gate, g_head, w_out, *, heads, dk, dv):
    m = xn.shape[0]
    bsz = m // seq
    proj = _matmul(xn, w_in, layer, n_out=2 * dk + 2 * dv, out_dtype=F32, name="gla_in")
    log_a = _gla_gate(xn, wg1_bf, wgate_bf, b_gate, layer)
    proj = proj.reshape(bsz, seq, -1)
    log_a = log_a.reshape(bsz, seq, dk)
    pad = (-seq) % GLA_CHUNK
    if pad:
        proj = jnp.pad(proj, ((0, 0), (0, pad), (0, 0)))
        log_a = jnp.pad(log_a, ((0, 0), (0, pad), (0, 0)))
    y, s_new = _gla_recurrence(proj, log_a, s0, s0_layer, g_head, layer, heads=heads, dk=dk, dv=dv)
    y = y[:, :seq].reshape(m, dv)
    return _matmul(y, w_out, layer, n_out=w_out.shape[2], res=h, name="gla_out"), s_new


def _sb_prompt_body(bias_ref, q_ref, k_ref, v_ref, o_ref, kb_scr, vb_scr, *, scale):
    h = pl.program_id(1)
    i = pl.program_id(2)
    blk, rows = SB_BLOCK, SB_ROWS
    n_sub = blk // rows
    hd = q_ref.shape[2]

    @pl.when(i == 0)
    def _():
        kb_scr[...] = k_ref[0].astype(BF16)
        vb_scr[...] = v_ref[0].astype(BF16)

    bias = bias_ref[h]
    r_io = lax.broadcasted_iota(jnp.int32, (blk, blk), 0)
    c_io = lax.broadcasted_iota(jnp.int32, (blk, blk), 1)
    upper = (r_io > c_io).astype(BF16)
    upper2 = jnp.concatenate([upper, upper], axis=0)
    sub_r = lax.broadcasted_iota(jnp.int32, (rows, blk), 0)
    sub_c = lax.broadcasted_iota(jnp.int32, (rows, blk), 1)

    def sweep(jobs):
        keys = [pl.ds(pl.multiple_of(kb * blk, blk), blk) for _, kb, _, _ in jobs]
        z = [_dot_nt(q_ref[0, r0:r0 + rows, :], kb_scr[ks, :]) * scale + bias for (r0, _, _, _), ks in zip(jobs, keys)]
        sp = [_softplus(zz) for zz in z]
        vis = [sub_c < sub_r + (r0 % blk) if m else None for r0, _, _, m in jobs]
        lk = [-s_ if v_ is None else jnp.where(v_, -s_, 0.0) for s_, v_ in zip(sp, vis)]
        later = [_dot(jnp.concatenate(_split2(l_), axis=1), upper2) for l_ in lk]
        out = []
        for (_, _, (tail, o), _), ks, zz, s_, v_, l_, lt in zip(jobs, keys, z, sp, vis, lk, later):
            a = jnp.exp((jnp.concatenate([tail] * (blk // LANES_V7X), axis=1) + lt) + (zz - s_))
            if v_ is not None:
                a = jnp.where(v_, a, 0.0)
            o = o + _dot(a.astype(BF16), vb_scr[ks, :])
            tail = tail + jnp.broadcast_to(lt[:, 0:1] + l_[:, 0:1], tail.shape)
            out.append((tail, o))
        return out

    zero = (jnp.zeros((rows, LANES_V7X), F32), jnp.zeros((rows, hd), F32))
    rows_a = [s * rows for s in range(n_sub)]
    rows_b = [blk + s * rows for s in range(n_sub)]
    st_b = sweep([(r, 2 * i + 1, zero, True) for r in rows_b])
    st = sweep([(r, 2 * i, zero, True) for r in rows_a] + [(r, 2 * i, s_, False) for r, s_ in zip(rows_b, st_b)])

    def step(jj, carry):
        kb = 2 * i - 1 - jj
        return tuple(sweep([(r, kb, s_, False) for r, s_ in zip(rows_a + rows_b, carry)]))

    final = lax.fori_loop(0, 2 * i, step, tuple(st))
    for r, (_, o) in zip(rows_a + rows_b, final):
        o_ref[0, r:r + rows, :] = o.astype(o_ref.dtype)


def _sb_prompt(q, k, v, bias, *, heads):
    bsz, seq, d = q.shape
    hd = d // heads
    tq = 2 * SB_BLOCK
    assert hd == LANES_V7X and seq % tq == 0
    body = functools.partial(_sb_prompt_body, scale=hd ** -0.5)
    return pl.pallas_call(
        body,
        grid=(bsz, heads, seq // tq),
        in_specs=[
            pl.BlockSpec(memory_space=pltpu.SMEM),
            pl.BlockSpec((1, tq, hd), lambda b, h, i: (b, i, h)),
            pl.BlockSpec((1, seq, hd), lambda b, h, i: (b, 0, h)),
            pl.BlockSpec((1, seq, hd), lambda b, h, i: (b, 0, h)),
        ],
        out_specs=pl.BlockSpec((1, tq, hd), lambda b, h, i: (b, i, h)),
        out_shape=jax.ShapeDtypeStruct((bsz, seq, d), BF16),
        scratch_shapes=[pltpu.VMEM((seq, hd), BF16), pltpu.VMEM((seq, hd), BF16)],
        compiler_params=_params("parallel", "parallel", "arbitrary"),
        name="sb_prompt",
    )(bias, q, k, v)


def _sb_sample_body(pt_ref, qbd_ref, bias_ref, kn_ref, vn_ref, kp_ref, vp_ref, o_ref, acc_scr, tail_scr,
                    *, heads, n_new, scale):
    del pt_ref
    p = pl.program_id(1)
    page, nl = kn_ref.shape[1], qbd_ref.shape[2]
    lower = (lax.broadcasted_iota(jnp.int32, (page, page), 1) > lax.broadcasted_iota(jnp.int32, (page, page), 0)).astype(BF16)

    def block(kb, vb, vis):
        z = _dot(kb, qbd_ref[0]) * scale + bias_ref[...]
        sp = _softplus(z)
        lk = -sp if vis is None else jnp.where(vis, -sp, 0.0)
        lk_hi, lk_mid, lk_lo = _split3(lk)
        later = _dot(lower, lk_hi) + _dot(lower, lk_mid) + _dot(lower, lk_lo)
        a = jnp.exp((tail_scr[0:1, :] + later) + (z - sp))
        if vis is not None:
            a = jnp.where(vis, a, 0.0)
        acc_scr[...] += _dot(jnp.transpose(a).astype(BF16), vb)
        tail_scr[0:1, :] = tail_scr[0:1, :] + jnp.sum(lk, axis=0, keepdims=True)

    def heads_to_lanes(ref):
        by_head = jnp.swapaxes(ref[0, 0], 0, 1)
        return jnp.concatenate([by_head[h].astype(BF16) for h in range(heads)], axis=1)

    @pl.when(p == 0)
    def _():
        acc_scr[...] = jnp.zeros_like(acc_scr)
        tail_scr[...] = jnp.zeros_like(tail_scr)
        key = lax.broadcasted_iota(jnp.int32, (page, nl), 0)
        tok = lax.broadcasted_iota(jnp.int32, (page, nl), 1) // heads
        block(kn_ref[0].astype(BF16), vn_ref[0].astype(BF16), (key < tok) & (key < n_new))

    @pl.when(p > 0)
    def _():
        block(heads_to_lanes(kp_ref), heads_to_lanes(vp_ref), None)

    @pl.when(p == pl.num_programs(1) - 1)
    def _():
        d = acc_scr.shape[1]
        hd = d // heads
        own = lax.broadcasted_iota(jnp.int32, (heads, d), 0) == lax.broadcasted_iota(jnp.int32, (heads, d), 1) // hd
        for t in range(n_new):
            rows = acc_scr[t * heads:(t + 1) * heads, :]
            o_ref[0, t:t + 1, :] = jnp.sum(jnp.where(own, rows, 0.0), axis=0, keepdims=True).astype(o_ref.dtype)


def _sb_sample(q, k_new, v_new, cache_k, cache_v, layer, page_table, bias, *, heads):
    bd, t_new, d = q.shape
    hd = d // heads
    page = cache_k.shape[2]
    n_pages = page_table.shape[1]
    nl = LANES_V7X
    assert heads * t_new <= nl and heads % 8 == 0 and hd % LANES_V7X == 0 and t_new <= page
    q4 = q.reshape(bd, t_new, heads, hd)
    qbd = jnp.einsum("bthx,hg->bhxtg", q4, jnp.eye(heads, dtype=q.dtype)).reshape(bd, d, t_new * heads)
    qbd = jnp.pad(qbd, ((0, 0), (0, 0), (0, nl - t_new * heads))).astype(BF16)
    bias_l = jnp.pad(jnp.tile(bias, t_new), (0, nl - t_new * heads)).reshape(1, nl)
    kn = jnp.pad(k_new, ((0, 0), (0, page - t_new), (0, 0)))
    vn = jnp.pad(v_new, ((0, 0), (0, page - t_new), (0, 0)))

    def page_map(b, p, pt):
        return (layer, pt[b, n_pages - jnp.maximum(p, 1)], 0, 0, 0)

    body = functools.partial(_sb_sample_body, heads=heads, n_new=t_new, scale=hd ** -0.5)
    grid_spec = pltpu.PrefetchScalarGridSpec(
        num_scalar_prefetch=1,
        grid=(bd, n_pages + 1),
        in_specs=[
            pl.BlockSpec((1, d, nl), lambda b, p, pt: (b, 0, 0)),
            pl.BlockSpec((1, nl), lambda b, p, pt: (0, 0)),
            pl.BlockSpec((1, page, d), lambda b, p, pt: (b, 0, 0)),
            pl.BlockSpec((1, page, d), lambda b, p, pt: (b, 0, 0)),
            pl.BlockSpec((1, 1, page, heads, hd), page_map),
            pl.BlockSpec((1, 1, page, heads, hd), page_map),
        ],
        out_specs=pl.BlockSpec((1, t_new, d), lambda b, p, pt: (b, 0, 0)),
        scratch_shapes=[pltpu.VMEM((nl, d), F32), pltpu.VMEM((8, nl), F32)],
    )
    return pl.pallas_call(
        body,
        grid_spec=grid_spec,
        out_shape=jax.ShapeDtypeStruct((bd, t_new, d), F32),
        compiler_params=_params("parallel", "arbitrary"),
        name="sb_sample",
    )(page_table, qbd, bias_l, kn, vn, cache_k, cache_v)


def _conv_prompt_body(x_ref, wb_ref, wc_ref, wh_ref, wconv_ref, buf0_ref, y_ref, st_ref, u_scr, *, blocks_per_seq):
    m = pl.program_id(1)
    tm = x_ref.shape[0]
    x = x_ref[...]
    bg = _dot(x, wb_ref[0].astype(BF16))
    u = _dot(x, wc_ref[0].astype(BF16)) * _dot(x, wh_ref[0].astype(BF16))

    @pl.when(m % blocks_per_seq == 0)
    def _():
        u_scr[6:8, :] = buf0_ref[0]

    u_scr[8:8 + tm, :] = u
    wconv = wconv_ref[0]
    conv = wconv[0:1, :] * u_scr[6:6 + tm, :] + wconv[1:2, :] * u_scr[7:7 + tm, :] + wconv[2:3, :] * u
    y_ref[...] = (bg * conv).astype(y_ref.dtype)
    st_ref[0] = u_scr[tm + 6:tm + 8, :]
    u_scr[0:8, :] = u_scr[tm:tm + 8, :]


def _conv_prompt(xn, w_in, w_conv, layer, buf0, seq):
    m, d = xn.shape
    tm = _tile(seq, 1024)
    tn = _tile(d, 256)
    nb = d // tn
    blocks_per_seq = seq // tm
    body = functools.partial(_conv_prompt_body, blocks_per_seq=blocks_per_seq)
    return pl.pallas_call(
        body,
        grid=(nb, m // tm),
        in_specs=[
            pl.BlockSpec((tm, d), lambda n, i: (i, 0)),
            pl.BlockSpec((1, d, tn), lambda n, i: (layer, 0, n)),
            pl.BlockSpec((1, d, tn), lambda n, i: (layer, 0, nb + n)),
            pl.BlockSpec((1, d, tn), lambda n, i: (layer, 0, 2 * nb + n)),
            pl.BlockSpec((1, w_conv.shape[1], tn), lambda n, i: (layer, 0, n)),
            pl.BlockSpec((1, 2, tn), lambda n, i: (i // blocks_per_seq, 0, n)),
        ],
        out_specs=[
            pl.BlockSpec((tm, tn), lambda n, i: (i, n)),
            pl.BlockSpec((1, 2, tn), lambda n, i: (i // blocks_per_seq, 0, n)),
        ],
        out_shape=[
            jax.ShapeDtypeStruct((m, d), BF16),
            jax.ShapeDtypeStruct((m // seq, 2, d), F32),
        ],
        scratch_shapes=[pltpu.VMEM((tm + 8, tn), F32)],
        compiler_params=_params("parallel", "arbitrary"),
        name="conv_prompt",
    )(xn, w_in, w_in, w_in, w_conv, buf0)


def _conv_sample_body(proj_ref, buf0_ref, wconv_ref, y_ref, st_ref, *, seq):
    d = y_ref.shape[2]
    taps = [buf0_ref[0], buf0_ref[1]]
    for t in range(seq):
        taps.append(proj_ref[t, :, d:2 * d] * proj_ref[t, :, 2 * d:3 * d])
    for t in range(seq):
        conv = wconv_ref[0:1, :] * taps[t] + wconv_ref[1:2, :] * taps[t + 1] + wconv_ref[2:3, :] * taps[t + 2]
        y_ref[t] = (proj_ref[t, :, 0:d] * conv).astype(y_ref.dtype)
    st_ref[0] = taps[seq]
    st_ref[1] = taps[seq + 1]


def _conv_sample(proj_tm, buf0_tm, w_conv):
    seq, bd, d3 = proj_tm.shape
    d = d3 // 3
    return pl.pallas_call(
        functools.partial(_conv_sample_body, seq=seq),
        out_shape=[jax.ShapeDtypeStruct((seq, bd, d), BF16), jax.ShapeDtypeStruct((2, bd, d), F32)],
        compiler_params=pltpu.CompilerParams(vmem_limit_bytes=VMEM_LIMIT_V7X),
        name="conv_sample",
    )(proj_tm, buf0_tm, w_conv)


def kernel(x_prompt, x_sample, state_gla, cache_sb_k, cache_sb_v, state_conv, cache_mem_k, cache_mem_v, page_table, mem_prompt, g_mix, g_xattn, g_mem, g_ffn, g_final, w_gla_in, w_gla_gate, b_gla_gate, g_gla_head, w_gla_out, w_sb_in, b_sb, w_sb_out, w_conv_in, w_conv, w_conv_out, w_xq, w_xkv, w_xo, w_ffn_in, w_ffn_out):
    bsz, seq, d = x_prompt.shape
    bd, t_new, _ = x_sample.shape
    depth = g_mix.shape[0]
    n_mem = mem_prompt.shape[1]
    gla_heads = state_gla.shape[2]
    dk = w_gla_gate.shape[2]
    dv = w_gla_out.shape[1]
    sb_heads = b_sb.shape[1]
    x_heads = cache_mem_k.shape[3]
    xw = w_xq.shape[2]
    mp, ms = bsz * seq, bd * t_new

    bf = lambda w: w.astype(BF16)
    rank = w_gla_gate.shape[1]
    w_gla_g1 = bf(jnp.pad(w_gla_in[:, :, 2 * dk + 2 * dv:], ((0, 0), (0, 0), (0, GLA_GATE_PAD - rank))))
    w_gla_gate_p = bf(jnp.pad(w_gla_gate, ((0, 0), (0, GLA_GATE_PAD - rank), (0, 0))))
    w_xq_bf, w_xkv_bf, w_xo_bf, w_ffn_out_bf = bf(w_xq), bf(w_xkv), bf(w_xo), bf(w_ffn_out)
    b_gla_gate3 = b_gla_gate.reshape(-1, 1, dk)
    g_gla_head3 = g_gla_head.reshape(-1, 1, dv // gla_heads)
    g_xattn3 = g_xattn.reshape(depth, 1, d)
    cache_mem_k3 = cache_mem_k.reshape(depth, bd * n_mem, xw)
    cache_mem_v3 = cache_mem_v.reshape(depth, bd * n_mem, xw)
    zero_state = jnp.zeros((1, bsz) + state_gla.shape[2:], F32)

    hp = x_prompt.reshape(mp, d)
    hs = x_sample.reshape(ms, d)
    mem_k, mem_v = _mem_kv(mem_prompt.reshape(bsz * n_mem, d), g_mem, w_xkv_bf)

    gla_p, gla_s, sbk_p, sbv_p, sbk_s, sbv_s, conv_p, conv_s = [], [], [], [], [], [], [], []
    for i in range(depth):
        kind, j = i % 3, i // 3
        xp = _rmsnorm(hp, g_mix[i], BF16)
        xs = _rmsnorm(hs, g_mix[i], BF16)
        if kind == 0:
            gla = functools.partial(_gla_mixer, layer=j, w_in=w_gla_in, wg1_bf=w_gla_g1, wgate_bf=w_gla_gate_p,
                                    b_gate=b_gla_gate3, g_head=g_gla_head3, w_out=w_gla_out,
                                    heads=gla_heads, dk=dk, dv=dv)
            hp, sp = gla(xp, hp, zero_state, 0, seq)
            hs, ss = gla(xs, hs, state_gla, j, t_new)
            gla_p.append(sp)
            gla_s.append(ss)
        elif kind == 1:
            qp = _matmul(xp, w_sb_in, j, n_out=d, col_off=0, out_dtype=BF16, name="sb_q")
            kp = _matmul(xp, w_sb_in, j, n_out=d, col_off=d, name="sb_k")
            vp = _matmul(xp, w_sb_in, j, n_out=d, col_off=2 * d, name="sb_v")
            op = _sb_prompt(qp.reshape(bsz, seq, d), kp.reshape(bsz, seq, d), vp.reshape(bsz, seq, d), b_sb[j],
                            heads=sb_heads)
            hp = _matmul(op.reshape(mp, d), w_sb_out, j, n_out=d, res=hp, name="sb_out")
            qkv = _matmul(xs, w_sb_in, j, n_out=3 * d, name="sb_qkv_sample")
            qs, kn, vn = (qkv[:, c * d:(c + 1) * d].reshape(bd, t_new, d) for c in range(3))
            os_ = _sb_sample(qs, kn, vn, cache_sb_k, cache_sb_v, j, page_table, b_sb[j], heads=sb_heads)
            hs = _matmul(os_.reshape(ms, d).astype(BF16), w_sb_out, j, n_out=d, res=hs, name="sb_out")
            sbk_p.append(kp)
            sbv_p.append(vp)
            sbk_s.append(kn)
            sbv_s.append(vn)
        else:
            yp, bp = _conv_prompt(xp, w_conv_in, w_conv, j, jnp.zeros((bsz, 2, d), F32), seq)
            hp = _matmul(yp, w_conv_out, j, n_out=d, res=hp, name="conv_out")
            proj_s = _matmul(xs, w_conv_in, j, n_out=3 * d, name="conv_in_sample")
            ys, bs = _conv_sample(proj_s.reshape(bd, t_new, 3 * d).transpose(1, 0, 2),
                                  state_conv[j].transpose(1, 0, 2), w_conv[j])
            hs = _matmul(ys.transpose(1, 0, 2).reshape(ms, d), w_conv_out, j, n_out=d, res=hs, name="conv_out")
            conv_p.append(bp)
            conv_s.append(bs.transpose(1, 0, 2))
        hp = _xattn(hp, g_xattn3, w_xq_bf, mem_k, mem_v, w_xo_bf, i,
                    heads=x_heads, rows_per_seq=seq, n_mem=n_mem, grouped=False)
        hs = _xattn(hs, g_xattn3, w_xq_bf, cache_mem_k3, cache_mem_v3, w_xo_bf, i,
                    heads=x_heads, rows_per_seq=t_new, n_mem=n_mem, grouped=True)
        ap = _swiglu_in(_rmsnorm(hp, g_ffn[i], BF16), w_ffn_in, i, tm_pref=2048, tf_pref=256)
        hp = _matmul(ap, w_ffn_out_bf, i, n_out=d, res=hp, tm_pref=512, tn_pref=256, name="ffn_out")
        as_ = _swiglu_in(_rmsnorm(hs, g_ffn[i], BF16), w_ffn_in, i, tm_pref=2048, tf_pref=256)
        hs = _matmul(as_, w_ffn_out_bf, i, n_out=d, res=hs, tm_pref=512, tn_pref=256, name="ffn_out")

    y_prompt = _rmsnorm(hp, g_final, F32).reshape(bsz, seq, d)
    y_sample = _rmsnorm(hs, g_final, F32).reshape(bd, t_new, d)
    hd = d // sb_heads
    sb_p = lambda xs_: jnp.stack(xs_).reshape(-1, bsz, seq, sb_heads, hd)
    sb_s = lambda xs_: jnp.stack(xs_).reshape(-1, bd, t_new, sb_heads, hd)
    xhd = xw // x_heads
    return (y_prompt, y_sample,
            jnp.stack(gla_p), jnp.stack(gla_s),
            sb_p(sbk_p), sb_p(sbv_p), sb_s(sbk_s), sb_s(sbv_s),
            jnp.stack(conv_p), jnp.stack(conv_s),
            mem_k.reshape(depth, bsz, n_mem, x_heads, xhd), mem_v.reshape(depth, bsz, n_mem, x_heads, xhd))
```

```python
import functools

import jax
import jax.numpy as jnp
from jax import lax
from jax.experimental import pallas as pl
from jax.experimental.pallas import tpu as pltpu

F32 = jnp.float32
BF16 = jnp.bfloat16

LANES_V7X = 128
VMEM_LIMIT_V7X = 56 * 1024 * 1024
RMS_EPS = 1e-6
GLA_TAU = 16.0
GLA_CHUNK = 32
GLA_GATE_PAD = LANES_V7X
GLA_HEAD_GROUP = 2
SB_BLOCK = 256
SB_ROWS = 128
NEG_BIG = -1e30


def _params(*sem):
    return pltpu.CompilerParams(dimension_semantics=sem, vmem_limit_bytes=VMEM_LIMIT_V7X)


def _tile(n, pref):
    if n <= pref:
        return n
    t = (pref // LANES_V7X) * LANES_V7X
    while t > LANES_V7X and n % t:
        t -= LANES_V7X
    assert n % t == 0, (n, pref)
    return t


def _softplus(z):
    return jnp.maximum(z, 0.0) + jnp.log(1.0 + jnp.exp(-jnp.abs(z)))


def _sigmoid(z):
    return 1.0 / (1.0 + jnp.exp(-z))


def _split2(x):
    hi = x.astype(BF16)
    lo = (x - hi.astype(F32)).astype(BF16)
    return hi, lo


def _split3(x):
    hi = x.astype(BF16)
    r1 = x - hi.astype(F32)
    mid = r1.astype(BF16)
    lo = (r1 - mid.astype(F32)).astype(BF16)
    return hi, mid, lo


def _dot(a, b):
    return jnp.dot(a, b, preferred_element_type=F32)


def _dot_nt(a, b):
    return lax.dot_general(a, b, (((1,), (1,)), ((), ())), preferred_element_type=F32)


def _dot_tn(a, b):
    return lax.dot_general(a, b, (((0,), (0,)), ((), ())), preferred_element_type=F32)


def _rmsnorm_body(x_ref, g_ref, o_ref):
    x = x_ref[...]
    ms = jnp.mean(x * x, axis=-1, keepdims=True)
    o_ref[...] = ((x * lax.rsqrt(ms + RMS_EPS)) * g_ref[...]).astype(o_ref.dtype)


def _rmsnorm(x, g, out_dtype):
    m, d = x.shape
    tm = _tile(m, 512)
    return pl.pallas_call(
        _rmsnorm_body,
        grid=(m // tm,),
        in_specs=[pl.BlockSpec((tm, d), lambda i: (i, 0)), pl.BlockSpec((1, d), lambda i: (0, 0))],
        out_specs=pl.BlockSpec((tm, d), lambda i: (i, 0)),
        out_shape=jax.ShapeDtypeStruct((m, d), out_dtype),
        compiler_params=_params("parallel"),
        name="rmsnorm",
    )(x, g.reshape(1, d))


def _mm_body(*refs, has_res, has_side, has_side_res, w_transposed):
    dot = _dot_nt if w_transposed else _dot
    it = iter(refs)
    x_ref, w_ref = next(it), next(it)
    r_ref = next(it) if has_res else None
    xs_ref = next(it) if has_side else None
    rs_ref = next(it) if has_side_res else None
    o_ref = next(it)
    os_ref = next(it) if has_side else None
    wb = w_ref[0].astype(BF16)
    acc = dot(x_ref[...], wb)
    if has_res:
        acc = acc + r_ref[...]
    o_ref[...] = acc.astype(o_ref.dtype)
    if has_side:
        @pl.when(pl.program_id(0) == 0)
        def _():
            side = dot(xs_ref[...], wb)
            if has_side_res:
                side = side + rs_ref[...]
            os_ref[...] = side.astype(os_ref.dtype)


def _side_col(i, j, last):
    return jnp.where(i == 0, j, last)


def _matmul(x, w, layer, *, n_out, col_off=0, res=None, out_dtype=F32, side=None, side_res=None, side_dtype=F32,
            tm_pref=1024, tn_pref=512, single_buffer_x=False, w_transposed=False, name="matmul"):
    m, k = x.shape
    tm, tn = _tile(m, tm_pref), _tile(n_out, tn_pref)
    assert col_off % tn == 0
    off = col_off // tn
    nj = n_out // tn
    x_mode = dict(pipeline_mode=pl.Buffered(1)) if single_buffer_x else {}
    w_spec = (pl.BlockSpec((1, tn, k), lambda i, j: (layer, j + off, 0)) if w_transposed else
              pl.BlockSpec((1, k, tn), lambda i, j: (layer, 0, j + off)))
    in_specs = [pl.BlockSpec((tm, k), lambda i, j: (i, 0), **x_mode), w_spec]
    args = [x, w]
    out_specs = [pl.BlockSpec((tm, tn), lambda i, j: (i, j))]
    out_shape = [jax.ShapeDtypeStruct((m, n_out), out_dtype)]
    if res is not None:
        in_specs.append(pl.BlockSpec((tm, tn), lambda i, j: (i, j)))
        args.append(res)
    if side is not None:
        ms = side.shape[0]
        in_specs.append(pl.BlockSpec((ms, k), lambda i, j: (0, 0)))
        args.append(side)
        if side_res is not None:
            in_specs.append(pl.BlockSpec((ms, tn), lambda i, j: (0, _side_col(i, j, nj - 1))))
            args.append(side_res)
        out_specs.append(pl.BlockSpec((ms, tn), lambda i, j: (0, _side_col(i, j, nj - 1))))
        out_shape.append(jax.ShapeDtypeStruct((ms, n_out), side_dtype))
    body = functools.partial(_mm_body, has_res=res is not None, has_side=side is not None,
                             has_side_res=side_res is not None, w_transposed=w_transposed)
    outs = pl.pallas_call(
        body,
        grid=(m // tm, nj),
        in_specs=in_specs,
        out_specs=out_specs,
        out_shape=out_shape,
        compiler_params=_params("arbitrary", "arbitrary"),
        name=name,
    )(*args)
    return outs if side is not None else outs[0]


def _mem_kv_body(x_ref, g_ref, w_ref, k_ref, v_ref):
    x = x_ref[...]
    ms = jnp.mean(x * x, axis=-1, keepdims=True)
    xn = ((x * lax.rsqrt(ms + RMS_EPS)) * g_ref[0]).astype(BF16)
    kv = _dot(xn, w_ref[0])
    half = kv.shape[1] // 2
    k_ref[0] = kv[:, :half]
    v_ref[0] = kv[:, half:]


def _mem_kv(mem, g_mem, w_xkv_bf):
    rows, d = mem.shape
    depth, _, two_w = w_xkv_bf.shape
    xw = two_w // 2
    tm = _tile(rows, 512)
    out = jax.ShapeDtypeStruct((depth, rows, xw), F32)
    return pl.pallas_call(
        _mem_kv_body,
        grid=(depth, rows // tm),
        in_specs=[
            pl.BlockSpec((tm, d), lambda i, m: (m, 0)),
            pl.BlockSpec((1, 1, d), lambda i, m: (i, 0, 0)),
            pl.BlockSpec((1, d, two_w), lambda i, m: (i, 0, 0)),
        ],
        out_specs=[pl.BlockSpec((1, tm, xw), lambda i, m: (i, m, 0))] * 2,
        out_shape=[out, out],
        compiler_params=_params("parallel", "parallel"),
        name="mem_kv",
    )(mem, g_mem.reshape(depth, 1, d), w_xkv_bf)


def _xattn_body(x_ref, g_ref, wq_ref, mk_ref, mv_ref, wo_ref, o_ref, *, heads, scale, rows_per_group, mem_per_group):
    x = x_ref[...]
    ms = jnp.mean(x * x, axis=-1, keepdims=True)
    xn = ((x * lax.rsqrt(ms + RMS_EPS)) * g_ref[0]).astype(BF16)
    q = _dot(xn, wq_ref[0])
    mk = mk_ref[0].astype(BF16)
    mv = mv_ref[0].astype(BF16)
    hd = q.shape[1] // heads
    outs = []
    for h in range(heads):
        sl = slice(h * hd, (h + 1) * hd)
        s = _dot_nt(q[:, sl].astype(BF16), mk[:, sl]) * scale
        if rows_per_group is not None:
            rg = lax.broadcasted_iota(jnp.int32, s.shape, 0) // rows_per_group
            cg = lax.broadcasted_iota(jnp.int32, s.shape, 1) // mem_per_group
            s = jnp.where(rg == cg, s, NEG_BIG)
        e = jnp.exp(s - jnp.max(s, axis=-1, keepdims=True))
        p = e / jnp.sum(e, axis=-1, keepdims=True)
        outs.append(_dot(p.astype(BF16), mv[:, sl]))
    o = jnp.concatenate(outs, axis=-1).astype(BF16)
    o_ref[...] = _dot(o, wo_ref[0]) + x


def _xattn(h, g, wq_bf, mk, mv, wo_bf, layer, *, heads, rows_per_seq, n_mem, grouped):
    m, d = h.shape
    xw = wq_bf.shape[2]
    scale = (xw // heads) ** -0.5
    if grouped:
        tm, mem_rows = m, mk.shape[1]
        mem_map = lambda i: (layer, 0, 0)
        rpg = rows_per_seq
    else:
        tm, mem_rows = _tile(rows_per_seq, 256), n_mem
        blocks_per_seq = rows_per_seq // tm
        mem_map = lambda i: (layer, i // blocks_per_seq, 0)
        rpg = None
    body = functools.partial(_xattn_body, heads=heads, scale=scale, rows_per_group=rpg, mem_per_group=n_mem)
    return pl.pallas_call(
        body,
        grid=(m // tm,),
        in_specs=[
            pl.BlockSpec((tm, d), lambda i: (i, 0)),
            pl.BlockSpec((1, 1, d), lambda i: (layer, 0, 0)),
            pl.BlockSpec((1, d, xw), lambda i: (layer, 0, 0)),
            pl.BlockSpec((1, mem_rows, xw), mem_map),
            pl.BlockSpec((1, mem_rows, xw), mem_map),
            pl.BlockSpec((1, xw, d), lambda i: (layer, 0, 0)),
        ],
        out_specs=pl.BlockSpec((tm, d), lambda i: (i, 0)),
        out_shape=jax.ShapeDtypeStruct((m, d), F32),
        compiler_params=_params("parallel"),
        name="xattn",
    )(h, g, wq_bf, mk, mv, wo_bf)


def _swiglu_in_body(x_ref, xs_ref, wg_ref, wu_ref, o_ref, os_ref):
    wg = wg_ref[0].astype(BF16)
    wu = wu_ref[0].astype(BF16)

    def act(x):
        gate = _dot(x, wg)
        return (gate * _sigmoid(gate)) * _dot(x, wu)

    o_ref[...] = act(x_ref[...]).astype(o_ref.dtype)

    @pl.when(pl.program_id(0) == 0)
    def _():
        os_ref[...] = act(xs_ref[...]).astype(os_ref.dtype)


def _swiglu_in(xn, side, w_in, layer, *, tm_pref, tf_pref):
    m, d = xn.shape
    ms = side.shape[0]
    f = w_in.shape[2] // 2
    tm, tf = _tile(m, tm_pref), _tile(f, tf_pref)
    nf = f // tf
    return pl.pallas_call(
        _swiglu_in_body,
        grid=(m // tm, nf),
        in_specs=[
            pl.BlockSpec((tm, d), lambda i, j: (i, 0), pipeline_mode=pl.Buffered(1)),
            pl.BlockSpec((ms, d), lambda i, j: (0, 0)),
            pl.BlockSpec((1, d, tf), lambda i, j: (layer, 0, j)),
            pl.BlockSpec((1, d, tf), lambda i, j: (layer, 0, j + nf)),
        ],
        out_specs=[pl.BlockSpec((tm, tf), lambda i, j: (i, j)),
                   pl.BlockSpec((ms, tf), lambda i, j: (0, _side_col(i, j, nf - 1)))],
        out_shape=[jax.ShapeDtypeStruct((m, f), BF16), jax.ShapeDtypeStruct((ms, f), BF16)],
        compiler_params=_params("arbitrary", "arbitrary"),
        name="swiglu_in",
    )(xn, side, w_in, w_in)


def _gla_gate_body(x_ref, wg1_ref, wgate_ref, b_ref, o_ref):
    g1 = _dot(x_ref[...], wg1_ref[0])
    z = _dot(g1.astype(BF16), wgate_ref[0]) + b_ref[0]
    log_sig = jnp.minimum(z, 0.0) - jnp.log1p(jnp.exp(-jnp.abs(z)))
    o_ref[...] = log_sig / GLA_TAU


def _gla_gate(xn, wg1_bf, wgate_bf, b_gate, layer):
    m, d = xn.shape
    dk = wgate_bf.shape[2]
    tm = _tile(m, 512)
    return pl.pallas_call(
        _gla_gate_body,
        grid=(m // tm,),
        in_specs=[
            pl.BlockSpec((tm, d), lambda i: (i, 0)),
            pl.BlockSpec((1, d, GLA_GATE_PAD), lambda i: (layer, 0, 0)),
            pl.BlockSpec((1, GLA_GATE_PAD, dk), lambda i: (layer, 0, 0)),
            pl.BlockSpec((1, 1, dk), lambda i: (layer, 0, 0)),
        ],
        out_specs=pl.BlockSpec((tm, dk), lambda i: (i, 0)),
        out_shape=jax.ShapeDtypeStruct((m, dk), F32),
        compiler_params=_params("parallel"),
        name="gla_gate",
    )(xn, wg1_bf, wgate_bf, b_gate)


def _gla_body(q_ref, k_ref, v_ref, r_ref, la_ref, s0_ref, gh_ref, y_ref, sout_ref, s_scr, *, chunk, n_chunks, q_scale):
    t = pl.program_id(2)
    group, dkh, dvh = s_scr.shape

    @pl.when(t == 0)
    def _():
        for g in range(group):
            s_scr[g] = s0_ref[0, 0, g]

    row = lax.broadcasted_iota(jnp.int32, (chunk, chunk), 0)
    col = lax.broadcasted_iota(jnp.int32, (chunk, chunk), 1)
    causal = row >= col
    tri = causal.astype(BF16)
    gh = gh_ref[0]
    ksl = [slice(g * dkh, (g + 1) * dkh) for g in range(group)]
    vsl = [slice(g * dvh, (g + 1) * dvh) for g in range(group)]

    def step(c, carry):
        rows = pl.ds(pl.multiple_of(c * chunk, chunk), chunk)
        g_hi, g_mid, g_lo = _split3(la_ref[0, rows, :])
        b = _dot(tri, g_hi) + _dot(tri, g_mid) + _dot(tri, g_lo)
        b_last = b[chunk - 1:chunk, :]
        qf = q_ref[0, rows, :] * q_scale
        kf = k_ref[0, rows, :]
        vb = v_ref[0, rows, :].astype(BF16)
        q_dec = (qf * jnp.exp(b)).astype(BF16)
        k_inv = (kf * jnp.exp(-b)).astype(BF16)
        k_dec = (kf * jnp.exp(b_last - b)).astype(BF16)
        decay = jnp.exp(b_last)
        att = [_dot_nt(q_dec[:, ks], k_inv[:, ks]) for ks in ksl]
        att = [jnp.where(causal, a, 0.0).astype(BF16) for a in att]
        s = [s_scr[g] for g in range(group)]
        o = [_dot(q_dec[:, ks], s_.astype(BF16)) + _dot(a, vb[:, vs]) for ks, vs, s_, a in zip(ksl, vsl, s, att)]
        upd = [_dot_tn(k_dec[:, ks], vb[:, vs]) for ks, vs in zip(ksl, vsl)]
        for g in range(group):
            decay_col = jnp.transpose(jnp.broadcast_to(decay[:, ksl[g]], (LANES_V7X, dkh)))
            s_scr[g] = s[g] * jnp.tile(decay_col, (1, dvh // LANES_V7X)) + upd[g]
        for g in range(group):
            ms = jnp.mean(o[g] * o[g], axis=-1, keepdims=True)
            on = (o[g] * lax.rsqrt(ms + RMS_EPS)) * gh
            rr = r_ref[0, rows, vsl[g]]
            y_ref[0, rows, vsl[g]] = (on * (rr * _sigmoid(rr))).astype(y_ref.dtype)
        return carry

    lax.fori_loop(0, n_chunks, step, 0)

    @pl.when(t == pl.num_programs(2) - 1)
    def _():
        for g in range(group):
            sout_ref[0, g] = s_scr[g]


def _gla_recurrence(proj, log_a, s0, s0_layer, g_head, layer, *, heads, dk, dv):
    bsz, seq, _ = proj.shape
    dkh, dvh = dk // heads, dv // heads
    group = GLA_HEAD_GROUP
    tt = _tile(seq, 256)
    assert tt % GLA_CHUNK == 0 and heads % group == 0 and (2 * dk) % (group * dvh) == 0
    hg = heads // group
    v_off = 2 * dk // (group * dvh)
    body = functools.partial(_gla_body, chunk=GLA_CHUNK, n_chunks=tt // GLA_CHUNK, q_scale=dkh ** -0.5)
    return pl.pallas_call(
        body,
        grid=(bsz, hg, seq // tt),
        in_specs=[
            pl.BlockSpec((1, tt, group * dkh), lambda b, h, t: (b, t, h)),
            pl.BlockSpec((1, tt, group * dkh), lambda b, h, t: (b, t, hg + h)),
            pl.BlockSpec((1, tt, group * dvh), lambda b, h, t: (b, t, v_off + h)),
            pl.BlockSpec((1, tt, group * dvh), lambda b, h, t: (b, t, v_off + hg + h)),
            pl.BlockSpec((1, tt, group * dkh), lambda b, h, t: (b, t, h)),
            pl.BlockSpec((1, 1, group, dkh, dvh), lambda b, h, t: (s0_layer, b, h, 0, 0)),
            pl.BlockSpec((1, 1, dvh), lambda b, h, t: (layer, 0, 0)),
        ],
        out_specs=[
            pl.BlockSpec((1, tt, group * dvh), lambda b, h, t: (b, t, h)),
            pl.BlockSpec((1, group, dkh, dvh), lambda b, h, t: (b, h, 0, 0)),
        ],
        out_shape=[
            jax.ShapeDtypeStruct((bsz, seq, dv), BF16),
            jax.ShapeDtypeStruct((bsz, heads, dkh, dvh), F32),
        ],
        scratch_shapes=[pltpu.VMEM((group, dkh, dvh), F32)],
        compiler_params=_params("parallel", "parallel", "arbitrary"),
        name="gla_recurrence",
    )(proj, proj, proj, proj, log_a, s0, g_head)


def _gla_mixer(xp, xs, hp, hs, state, layer, seq_p, seq_s, w_in, wg1_bf, wgate_bf, b_gate, g_head, w_out, *, heads, dk, dv):
    proj_p, proj_s = _matmul(xp, jnp.swapaxes(w_in, 1, 2), layer, n_out=2 * dk + 2 * dv, side=xs,
                             w_transposed=True, name="gla_in")

    def recur(xn, proj, seq, s0, s0_layer):
        bsz = xn.shape[0] // seq
        log_a = _gla_gate(xn, wg1_bf, wgate_bf, b_gate, layer).reshape(bsz, seq, dk)
        proj = proj.reshape(bsz, seq, -1)
        pad = (-seq) % GLA_CHUNK
        if pad:
            proj = jnp.pad(proj, ((0, 0), (0, pad), (0, 0)))
            log_a = jnp.pad(log_a, ((0, 0), (0, pad), (0, 0)))
        y, s_new = _gla_recurrence(proj, log_a, s0, s0_layer, g_head, layer, heads=heads, dk=dk, dv=dv)
        return y[:, :seq].reshape(xn.shape[0], dv), s_new

    zero_state = jnp.zeros((1, xp.shape[0] // seq_p) + state.shape[2:], F32)
    yp, sp = recur(xp, proj_p, seq_p, zero_state, 0)
    ys, ss = recur(xs, proj_s, seq_s, state, layer)
    hp, hs = _matmul(yp, w_out, layer, n_out=w_out.shape[2], res=hp, side=ys, side_res=hs, name="gla_out")
    return hp, hs, sp, ss


def _sb_prompt_body(bias_ref, q_ref, k_ref, v_ref, o_ref, kb_scr, vb_scr, *, scale):
    h = pl.program_id(1)
    i = pl.program_id(2)
    blk, rows = SB_BLOCK, SB_ROWS
    n_sub = blk // rows
    hd = q_ref.shape[2]

    @pl.when(i == 0)
    def _():
        kb_scr[...] = k_ref[0].astype(BF16)
        vb_scr[...] = v_ref[0].astype(BF16)

    bias = bias_ref[h]
    r_io = lax.broadcasted_iota(jnp.int32, (blk, blk), 0)
    c_io = lax.broadcasted_iota(jnp.int32, (blk, blk), 1)
    upper = jnp.where(r_io > c_io, -1.0, 0.0).astype(BF16)
    upper2 = jnp.concatenate([upper, upper], axis=0)
    sub_r = lax.broadcasted_iota(jnp.int32, (rows, blk), 0)
    sub_c = lax.broadcasted_iota(jnp.int32, (rows, blk), 1)

    def sweep(jobs):
        keys = [pl.ds(pl.multiple_of(kb * blk, blk), blk) for _, kb, _, _ in jobs]
        z = [_dot_nt(q_ref[0, r0:r0 + rows, :], kb_scr[ks, :]) * scale + bias for (r0, _, _, _), ks in zip(jobs, keys)]
        sp = [_softplus(zz) for zz in z]
        vis = [sub_c < sub_r + (r0 % blk) if m else None for r0, _, _, m in jobs]
        nlk = [s_ if v_ is None else jnp.where(v_, s_, 0.0) for s_, v_ in zip(sp, vis)]
        later = [_dot(jnp.concatenate(_split2(l_), axis=1), upper2) for l_ in nlk]
        out = []
        for (_, _, (tail, o), _), ks, zz, s_, v_, l_, lt in zip(jobs, keys, z, sp, vis, nlk, later):
            a = jnp.exp((jnp.concatenate([tail] * (blk // LANES_V7X), axis=1) + lt) + (zz - s_))
            if v_ is not None:
                a = jnp.where(v_, a, 0.0)
            o = o + _dot(a.astype(BF16), vb_scr[ks, :])
            tail = tail + jnp.broadcast_to(lt[:, 0:1] - l_[:, 0:1], tail.shape)
            out.append((tail, o))
        return out

    zero = (jnp.zeros((rows, LANES_V7X), F32), jnp.zeros((rows, hd), F32))
    rows_a = [s * rows for s in range(n_sub)]
    rows_b = [blk + s * rows for s in range(n_sub)]
    st_b = sweep([(r, 2 * i + 1, zero, True) for r in rows_b])
    st = sweep([(r, 2 * i, zero, True) for r in rows_a] + [(r, 2 * i, s_, False) for r, s_ in zip(rows_b, st_b)])

    def step(jj, carry):
        kb = 2 * i - 1 - jj
        return tuple(sweep([(r, kb, s_, False) for r, s_ in zip(rows_a + rows_b, carry)]))

    final = lax.fori_loop(0, 2 * i, step, tuple(st))
    for r, (_, o) in zip(rows_a + rows_b, final):
        o_ref[0, r:r + rows, :] = o.astype(o_ref.dtype)


def _sb_prompt(q, k, v, bias, *, heads):
    bsz, seq, d = q.shape
    hd = d // heads
    tq = 2 * SB_BLOCK
    assert hd == LANES_V7X and seq % tq == 0
    body = functools.partial(_sb_prompt_body, scale=hd ** -0.5)
    return pl.pallas_call(
        body,
        grid=(bsz, heads, seq // tq),
        in_specs=[
            pl.BlockSpec(memory_space=pltpu.SMEM),
            pl.BlockSpec((1, tq, hd), lambda b, h, i: (b, i, h)),
            pl.BlockSpec((1, seq, hd), lambda b, h, i: (b, 0, h)),
            pl.BlockSpec((1, seq, hd), lambda b, h, i: (b, 0, h)),
        ],
        out_specs=pl.BlockSpec((1, tq, hd), lambda b, h, i: (b, i, h)),
        out_shape=jax.ShapeDtypeStruct((bsz, seq, d), BF16),
        scratch_shapes=[pltpu.VMEM((seq, hd), BF16), pltpu.VMEM((seq, hd), BF16)],
        compiler_params=_params("parallel", "parallel", "arbitrary"),
        name="sb_prompt",
    )(bias, q, k, v)


def _sb_sample_body(pt_ref, qbd_ref, bias_ref, kn_ref, vn_ref, kp_ref, vp_ref, o_ref, acc_scr, tail_scr,
                    *, heads, n_new, scale):
    del pt_ref
    p = pl.program_id(1)
    page, nl = kn_ref.shape[1], qbd_ref.shape[2]
    lower = (lax.broadcasted_iota(jnp.int32, (page, page), 1) > lax.broadcasted_iota(jnp.int32, (page, page), 0)).astype(BF16)

    def block(kb, vb, vis):
        z = _dot(kb, qbd_ref[0]) * scale + bias_ref[...]
        sp = _softplus(z)
        lk = -sp if vis is None else jnp.where(vis, -sp, 0.0)
        lk_hi, lk_mid, lk_lo = _split3(lk)
        later = _dot(lower, lk_hi) + _dot(lower, lk_mid) + _dot(lower, lk_lo)
        a = jnp.exp((tail_scr[0:1, :] + later) + (z - sp))
        if vis is not None:
            a = jnp.where(vis, a, 0.0)
        acc_scr[...] += _dot(jnp.transpose(a).astype(BF16), vb)
        tail_scr[0:1, :] = tail_scr[0:1, :] + jnp.sum(lk, axis=0, keepdims=True)

    def heads_to_lanes(ref):
        by_head = jnp.swapaxes(ref[0, 0], 0, 1)
        return jnp.concatenate([by_head[h].astype(BF16) for h in range(heads)], axis=1)

    @pl.when(p == 0)
    def _():
        acc_scr[...] = jnp.zeros_like(acc_scr)
        tail_scr[...] = jnp.zeros_like(tail_scr)
        key = lax.broadcasted_iota(jnp.int32, (page, nl), 0)
        tok = lax.broadcasted_iota(jnp.int32, (page, nl), 1) // heads
        block(kn_ref[0].astype(BF16), vn_ref[0].astype(BF16), (key < tok) & (key < n_new))

    @pl.when(p > 0)
    def _():
        block(heads_to_lanes(kp_ref), heads_to_lanes(vp_ref), None)

    @pl.when(p == pl.num_programs(1) - 1)
    def _():
        d = acc_scr.shape[1]
        hd = d // heads
        own = lax.broadcasted_iota(jnp.int32, (heads, d), 0) == lax.broadcasted_iota(jnp.int32, (heads, d), 1) // hd
        for t in range(n_new):
            rows = acc_scr[t * heads:(t + 1) * heads, :]
            o_ref[0, t:t + 1, :] = jnp.sum(jnp.where(own, rows, 0.0), axis=0, keepdims=True).astype(o_ref.dtype)


def _sb_sample(q, k_new, v_new, cache_k, cache_v, layer, page_table, bias, *, heads):
    bd, t_new, d = q.shape
    hd = d // heads
    page = cache_k.shape[2]
    n_pages = page_table.shape[1]
    nl = LANES_V7X
    assert heads * t_new <= nl and heads % 8 == 0 and hd % LANES_V7X == 0 and t_new <= page
    q4 = q.reshape(bd, t_new, heads, hd)
    qbd = jnp.einsum("bthx,hg->bhxtg", q4, jnp.eye(heads, dtype=q.dtype)).reshape(bd, d, t_new * heads)
    qbd = jnp.pad(qbd, ((0, 0), (0, 0), (0, nl - t_new * heads))).astype(BF16)
    bias_l = jnp.pad(jnp.tile(bias, t_new), (0, nl - t_new * heads)).reshape(1, nl)
    kn = jnp.pad(k_new, ((0, 0), (0, page - t_new), (0, 0)))
    vn = jnp.pad(v_new, ((0, 0), (0, page - t_new), (0, 0)))

    def page_map(b, p, pt):
        return (layer, pt[b, n_pages - jnp.maximum(p, 1)], 0, 0, 0)

    body = functools.partial(_sb_sample_body, heads=heads, n_new=t_new, scale=hd ** -0.5)
    grid_spec = pltpu.PrefetchScalarGridSpec(
        num_scalar_prefetch=1,
        grid=(bd, n_pages + 1),
        in_specs=[
            pl.BlockSpec((1, d, nl), lambda b, p, pt: (b, 0, 0)),
            pl.BlockSpec((1, nl), lambda b, p, pt: (0, 0)),
            pl.BlockSpec((1, page, d), lambda b, p, pt: (b, 0, 0)),
            pl.BlockSpec((1, page, d), lambda b, p, pt: (b, 0, 0)),
            pl.BlockSpec((1, 1, page, heads, hd), page_map),
            pl.BlockSpec((1, 1, page, heads, hd), page_map),
        ],
        out_specs=pl.BlockSpec((1, t_new, d), lambda b, p, pt: (b, 0, 0)),
        scratch_shapes=[pltpu.VMEM((nl, d), F32), pltpu.VMEM((8, nl), F32)],
    )
    return pl.pallas_call(
        body,
        grid_spec=grid_spec,
        out_shape=jax.ShapeDtypeStruct((bd, t_new, d), F32),
        compiler_params=_params("parallel", "arbitrary"),
        name="sb_sample",
    )(page_table, qbd, bias_l, kn, vn, cache_k, cache_v)


def _conv_prompt_body(x_ref, wb_ref, wc_ref, wh_ref, wconv_ref, buf0_ref, y_ref, st_ref, u_scr, *, blocks_per_seq):
    m = pl.program_id(1)
    tm = x_ref.shape[0]
    x = x_ref[...]
    bg = _dot(x, wb_ref[0].astype(BF16))
    u = _dot(x, wc_ref[0].astype(BF16)) * _dot(x, wh_ref[0].astype(BF16))

    @pl.when(m % blocks_per_seq == 0)
    def _():
        u_scr[6:8, :] = buf0_ref[0]

    u_scr[8:8 + tm, :] = u
    wconv = wconv_ref[0]
    conv = wconv[0:1, :] * u_scr[6:6 + tm, :] + wconv[1:2, :] * u_scr[7:7 + tm, :] + wconv[2:3, :] * u
    y_ref[...] = (bg * conv).astype(y_ref.dtype)
    st_ref[0] = u_scr[tm + 6:tm + 8, :]
    u_scr[0:8, :] = u_scr[tm:tm + 8, :]


def _conv_prompt(xn, w_in, w_conv, layer, buf0, seq):
    m, d = xn.shape
    tm = _tile(seq, 1024)
    tn = _tile(d, 256)
    nb = d // tn
    blocks_per_seq = seq // tm
    body = functools.partial(_conv_prompt_body, blocks_per_seq=blocks_per_seq)
    return pl.pallas_call(
        body,
        grid=(nb, m // tm),
        in_specs=[
            pl.BlockSpec((tm, d), lambda n, i: (i, 0)),
            pl.BlockSpec((1, d, tn), lambda n, i: (layer, 0, n)),
            pl.BlockSpec((1, d, tn), lambda n, i: (layer, 0, nb + n)),
            pl.BlockSpec((1, d, tn), lambda n, i: (layer, 0, 2 * nb + n)),
            pl.BlockSpec((1, w_conv.shape[1], tn), lambda n, i: (layer, 0, n)),
            pl.BlockSpec((1, 2, tn), lambda n, i: (i // blocks_per_seq, 0, n)),
        ],
        out_specs=[
            pl.BlockSpec((tm, tn), lambda n, i: (i, n)),
            pl.BlockSpec((1, 2, tn), lambda n, i: (i // blocks_per_seq, 0, n)),
        ],
        out_shape=[
            jax.ShapeDtypeStruct((m, d), BF16),
            jax.ShapeDtypeStruct((m // seq, 2, d), F32),
        ],
        scratch_shapes=[pltpu.VMEM((tm + 8, tn), F32)],
        compiler_params=_params("parallel", "arbitrary"),
        name="conv_prompt",
    )(xn, w_in, w_in, w_in, w_conv, buf0)


def _conv_sample_body(proj_ref, buf0_ref, wconv_ref, y_ref, st_ref, *, seq):
    d = y_ref.shape[2]
    taps = [buf0_ref[0], buf0_ref[1]]
    for t in range(seq):
        taps.append(proj_ref[t, :, d:2 * d] * proj_ref[t, :, 2 * d:3 * d])
    for t in range(seq):
        conv = wconv_ref[0:1, :] * taps[t] + wconv_ref[1:2, :] * taps[t + 1] + wconv_ref[2:3, :] * taps[t + 2]
        y_ref[t] = (proj_ref[t, :, 0:d] * conv).astype(y_ref.dtype)
    st_ref[0] = taps[seq]
    st_ref[1] = taps[seq + 1]


def _conv_sample(proj_tm, buf0_tm, w_conv):
    seq, bd, d3 = proj_tm.shape
    d = d3 // 3
    return pl.pallas_call(
        functools.partial(_conv_sample_body, seq=seq),
        out_shape=[jax.ShapeDtypeStruct((seq, bd, d), BF16), jax.ShapeDtypeStruct((2, bd, d), F32)],
        compiler_params=pltpu.CompilerParams(vmem_limit_bytes=VMEM_LIMIT_V7X),
        name="conv_sample",
    )(proj_tm, buf0_tm, w_conv)


def kernel(x_prompt, x_sample, state_gla, cache_sb_k, cache_sb_v, state_conv, cache_mem_k, cache_mem_v, page_table, mem_prompt, g_mix, g_xattn, g_mem, g_ffn, g_final, w_gla_in, w_gla_gate, b_gla_gate, g_gla_head, w_gla_out, w_sb_in, b_sb, w_sb_out, w_conv_in, w_conv, w_conv_out, w_xq, w_xkv, w_xo, w_ffn_in, w_ffn_out):
    bsz, seq, d = x_prompt.shape
    bd, t_new, _ = x_sample.shape
    depth = g_mix.shape[0]
    n_mem = mem_prompt.shape[1]
    gla_heads = state_gla.shape[2]
    dk = w_gla_gate.shape[2]
    dv = w_gla_out.shape[1]
    sb_heads = b_sb.shape[1]
    x_heads = cache_mem_k.shape[3]
    xw = w_xq.shape[2]
    mp, ms = bsz * seq, bd * t_new

    bf = lambda w: w.astype(BF16)
    rank = w_gla_gate.shape[1]
    w_gla_g1 = bf(jnp.pad(w_gla_in[:, :, 2 * dk + 2 * dv:], ((0, 0), (0, 0), (0, GLA_GATE_PAD - rank))))
    w_gla_gate_p = bf(jnp.pad(w_gla_gate, ((0, 0), (0, GLA_GATE_PAD - rank), (0, 0))))
    w_xq_bf, w_xkv_bf, w_xo_bf, w_ffn_out_bf = bf(w_xq), bf(w_xkv), bf(w_xo), bf(w_ffn_out)
    b_gla_gate3 = b_gla_gate.reshape(-1, 1, dk)
    g_gla_head3 = g_gla_head.reshape(-1, 1, dv // gla_heads)
    g_xattn3 = g_xattn.reshape(depth, 1, d)
    cache_mem_k3 = cache_mem_k.reshape(depth, bd * n_mem, xw)
    cache_mem_v3 = cache_mem_v.reshape(depth, bd * n_mem, xw)

    hp = x_prompt.reshape(mp, d)
    hs = x_sample.reshape(ms, d)
    mem_k, mem_v = _mem_kv(mem_prompt.reshape(bsz * n_mem, d), g_mem, w_xkv_bf)

    gla_p, gla_s, sbk_p, sbv_p, sbk_s, sbv_s, conv_p, conv_s = [], [], [], [], [], [], [], []
    for i in range(depth):
        kind, j = i % 3, i // 3
        xp = _rmsnorm(hp, g_mix[i], BF16)
        xs = _rmsnorm(hs, g_mix[i], BF16)
        if kind == 0:
            hp, hs, sp, ss = _gla_mixer(xp, xs, hp, hs, state_gla, j, seq, t_new, w_gla_in, w_gla_g1, w_gla_gate_p,
                                        b_gla_gate3, g_gla_head3, w_gla_out, heads=gla_heads, dk=dk, dv=dv)
            gla_p.append(sp)
            gla_s.append(ss)
        elif kind == 1:
            qp, qs = _matmul(xp, w_sb_in, j, n_out=d, col_off=0, out_dtype=BF16, side=xs, name="sb_q")
            kp, kn = _matmul(xp, w_sb_in, j, n_out=d, col_off=d, side=xs, name="sb_k")
            vp, vn = _matmul(xp, w_sb_in, j, n_out=d, col_off=2 * d, side=xs, name="sb_v")
            qs, kn, vn = (a.reshape(bd, t_new, d) for a in (qs, kn, vn))
            op = _sb_prompt(qp.reshape(bsz, seq, d), kp.reshape(bsz, seq, d), vp.reshape(bsz, seq, d), b_sb[j],
                            heads=sb_heads)
            os_ = _sb_sample(qs, kn, vn, cache_sb_k, cache_sb_v, j, page_table, b_sb[j], heads=sb_heads)
            hp, hs = _matmul(op.reshape(mp, d), w_sb_out, j, n_out=d, res=hp,
                             side=os_.reshape(ms, d).astype(BF16), side_res=hs, name="sb_out")
            sbk_p.append(kp)
            sbv_p.append(vp)
            sbk_s.append(kn)
            sbv_s.append(vn)
        else:
            yp, bp = _conv_prompt(xp, w_conv_in, w_conv, j, jnp.zeros((bsz, 2, d), F32), seq)
            proj_s = _matmul(xs, w_conv_in, j, n_out=3 * d, name="conv_in_sample")
            ys, bs = _conv_sample(proj_s.reshape(bd, t_new, 3 * d).transpose(1, 0, 2),
                                  state_conv[j].transpose(1, 0, 2), w_conv[j])
            hp, hs = _matmul(yp, w_conv_out, j, n_out=d, res=hp,
                             side=ys.transpose(1, 0, 2).reshape(ms, d), side_res=hs, name="conv_out")
            conv_p.append(bp)
            conv_s.append(bs.transpose(1, 0, 2))
        hp = _xattn(hp, g_xattn3, w_xq_bf, mem_k, mem_v, w_xo_bf, i,
                    heads=x_heads, rows_per_seq=seq, n_mem=n_mem, grouped=False)
        hs = _xattn(hs, g_xattn3, w_xq_bf, cache_mem_k3, cache_mem_v3, w_xo_bf, i,
                    heads=x_heads, rows_per_seq=t_new, n_mem=n_mem, grouped=True)
        ap, as_ = _swiglu_in(_rmsnorm(hp, g_ffn[i], BF16), _rmsnorm(hs, g_ffn[i], BF16), w_ffn_in, i,
                             tm_pref=2048, tf_pref=256)
        hp, hs = _matmul(ap, w_ffn_out_bf, i, n_out=d, res=hp, side=as_, side_res=hs,
                         tm_pref=1024, tn_pref=256, single_buffer_x=True, name="ffn_out")

    y_prompt = _rmsnorm(hp, g_final, F32).reshape(bsz, seq, d)
    y_sample = _rmsnorm(hs, g_final, F32).reshape(bd, t_new, d)
    hd = d // sb_heads
    sb_p = lambda xs_: jnp.stack(xs_).reshape(-1, bsz, seq, sb_heads, hd)
    sb_s = lambda xs_: jnp.stack(xs_).reshape(-1, bd, t_new, sb_heads, hd)
    xhd = xw // x_heads
    return (y_prompt, y_sample,
            jnp.stack(gla_p), jnp.stack(gla_s),
            sb_p(sbk_p), sb_p(sbv_p), sb_s(sbk_s), sb_s(sbv_s),
            jnp.stack(conv_p), jnp.stack(conv_s),
            mem_k.reshape(depth, bsz, n_mem, x_heads, xhd), mem_v.reshape(depth, bsz, n_mem, x_heads, xhd))
```

```python
import functools

import jax
import jax.numpy as jnp
from jax import lax
from jax.experimental import pallas as pl
from jax.experimental.pallas import tpu as pltpu

F32 = jnp.float32
BF16 = jnp.bfloat16

LANES_V7X = 128
VMEM_LIMIT_V7X = 56 * 1024 * 1024
RMS_EPS = 1e-6
GLA_TAU = 16.0
GLA_CHUNK = 32
GLA_GATE_PAD = LANES_V7X
GLA_HEAD_GROUP = 2
SB_BLOCK = 256
SB_ROWS = 128
NEG_BIG = -1e30


def _params(*sem):
    return pltpu.CompilerParams(dimension_semantics=sem, vmem_limit_bytes=VMEM_LIMIT_V7X)


def _tile(n, pref):
    if n <= pref:
        return n
    t = (pref // LANES_V7X) * LANES_V7X
    while t > LANES_V7X and n % t:
        t -= LANES_V7X
    assert n % t == 0, (n, pref)
    return t


def _softplus(z):
    return jnp.maximum(z, 0.0) + jnp.log(1.0 + jnp.exp(-jnp.abs(z)))


def _sigmoid(z):
    return 1.0 / (1.0 + jnp.exp(-z))


def _split2(x):
    hi = x.astype(BF16)
    lo = (x - hi.astype(F32)).astype(BF16)
    return hi, lo


def _split3(x):
    hi = x.astype(BF16)
    r1 = x - hi.astype(F32)
    mid = r1.astype(BF16)
    lo = (r1 - mid.astype(F32)).astype(BF16)
    return hi, mid, lo


def _dot(a, b):
    return jnp.dot(a, b, preferred_element_type=F32)


def _dot_nt(a, b):
    return lax.dot_general(a, b, (((1,), (1,)), ((), ())), preferred_element_type=F32)


def _dot_tn(a, b):
    return lax.dot_general(a, b, (((0,), (0,)), ((), ())), preferred_element_type=F32)


def _rmsnorm_body(x_ref, g_ref, o_ref):
    x = x_ref[...]
    ms = jnp.mean(x * x, axis=-1, keepdims=True)
    o_ref[...] = ((x * lax.rsqrt(ms + RMS_EPS)) * g_ref[...]).astype(o_ref.dtype)


def _rmsnorm(x, g, out_dtype):
    m, d = x.shape
    tm = _tile(m, 512)
    return pl.pallas_call(
        _rmsnorm_body,
        grid=(m // tm,),
        in_specs=[pl.BlockSpec((tm, d), lambda i: (i, 0)), pl.BlockSpec((1, d), lambda i: (0, 0))],
        out_specs=pl.BlockSpec((tm, d), lambda i: (i, 0)),
        out_shape=jax.ShapeDtypeStruct((m, d), out_dtype),
        compiler_params=_params("parallel"),
        name="rmsnorm",
    )(x, g.reshape(1, d))


def _mm_body(*refs, has_res, has_side, has_side_res, w_transposed):
    dot = _dot_nt if w_transposed else _dot
    it = iter(refs)
    x_ref, w_ref = next(it), next(it)
    r_ref = next(it) if has_res else None
    xs_ref = next(it) if has_side else None
    rs_ref = next(it) if has_side_res else None
    o_ref = next(it)
    os_ref = next(it) if has_side else None
    wb = w_ref[0].astype(BF16)
    acc = dot(x_ref[...], wb)
    if has_res:
        acc = acc + r_ref[...]
    o_ref[...] = acc.astype(o_ref.dtype)
    if has_side:
        @pl.when(pl.program_id(0) == 0)
        def _():
            side = dot(xs_ref[...], wb)
            if has_side_res:
                side = side + rs_ref[...]
            os_ref[...] = side.astype(os_ref.dtype)


def _side_col(i, j, last):
    return jnp.where(i == 0, j, last)


def _matmul(x, w, layer, *, n_out, col_off=0, res=None, out_dtype=F32, side=None, side_res=None, side_dtype=F32,
            tm_pref=1024, tn_pref=512, single_buffer_x=False, w_transposed=False, name="matmul"):
    m, k = x.shape
    tm, tn = _tile(m, tm_pref), _tile(n_out, tn_pref)
    assert col_off % tn == 0
    off = col_off // tn
    nj = n_out // tn
    x_mode = dict(pipeline_mode=pl.Buffered(1)) if single_buffer_x else {}
    w_spec = (pl.BlockSpec((1, tn, k), lambda i, j: (layer, j + off, 0)) if w_transposed else
              pl.BlockSpec((1, k, tn), lambda i, j: (layer, 0, j + off)))
    in_specs = [pl.BlockSpec((tm, k), lambda i, j: (i, 0), **x_mode), w_spec]
    args = [x, w]
    out_specs = [pl.BlockSpec((tm, tn), lambda i, j: (i, j))]
    out_shape = [jax.ShapeDtypeStruct((m, n_out), out_dtype)]
    if res is not None:
        in_specs.append(pl.BlockSpec((tm, tn), lambda i, j: (i, j)))
        args.append(res)
    if side is not None:
        ms = side.shape[0]
        in_specs.append(pl.BlockSpec((ms, k), lambda i, j: (0, 0)))
        args.append(side)
        if side_res is not None:
            in_specs.append(pl.BlockSpec((ms, tn), lambda i, j: (0, _side_col(i, j, nj - 1))))
            args.append(side_res)
        out_specs.append(pl.BlockSpec((ms, tn), lambda i, j: (0, _side_col(i, j, nj - 1))))
        out_shape.append(jax.ShapeDtypeStruct((ms, n_out), side_dtype))
    body = functools.partial(_mm_body, has_res=res is not None, has_side=side is not None,
                             has_side_res=side_res is not None, w_transposed=w_transposed)
    outs = pl.pallas_call(
        body,
        grid=(m // tm, nj),
        in_specs=in_specs,
        out_specs=out_specs,
        out_shape=out_shape,
        compiler_params=_params("arbitrary", "arbitrary"),
        name=name,
    )(*args)
    return outs if side is not None else outs[0]


def _mem_kv_body(x_ref, g_ref, w_ref, k_ref, v_ref):
    x = x_ref[...]
    ms = jnp.mean(x * x, axis=-1, keepdims=True)
    xn = ((x * lax.rsqrt(ms + RMS_EPS)) * g_ref[0]).astype(BF16)
    kv = _dot(xn, w_ref[0])
    half = kv.shape[1] // 2
    k_ref[0] = kv[:, :half]
    v_ref[0] = kv[:, half:]


def _mem_kv(mem, g_mem, w_xkv_bf):
    rows, d = mem.shape
    depth, _, two_w = w_xkv_bf.shape
    xw = two_w // 2
    tm = _tile(rows, 512)
    out = jax.ShapeDtypeStruct((depth, rows, xw), F32)
    return pl.pallas_call(
        _mem_kv_body,
        grid=(depth, rows // tm),
        in_specs=[
            pl.BlockSpec((tm, d), lambda i, m: (m, 0)),
            pl.BlockSpec((1, 1, d), lambda i, m: (i, 0, 0)),
            pl.BlockSpec((1, d, two_w), lambda i, m: (i, 0, 0)),
        ],
        out_specs=[pl.BlockSpec((1, tm, xw), lambda i, m: (i, m, 0))] * 2,
        out_shape=[out, out],
        compiler_params=_params("parallel", "parallel"),
        name="mem_kv",
    )(mem, g_mem.reshape(depth, 1, d), w_xkv_bf)


def _xattn_body(x_ref, g_ref, g_next_ref, wq_ref, mk_ref, mv_ref, wo_ref, o_ref, on_ref, *, heads, scale,
                rows_per_group, mem_per_group):
    x = x_ref[...]
    ms = jnp.mean(x * x, axis=-1, keepdims=True)
    xn = ((x * lax.rsqrt(ms + RMS_EPS)) * g_ref[0]).astype(BF16)
    q = _dot(xn, wq_ref[0])
    mk = mk_ref[0].astype(BF16)
    mv = mv_ref[0].astype(BF16)
    hd = q.shape[1] // heads
    outs = []
    for h in range(heads):
        sl = slice(h * hd, (h + 1) * hd)
        s = _dot_nt(q[:, sl].astype(BF16), mk[:, sl]) * scale
        if rows_per_group is not None:
            rg = lax.broadcasted_iota(jnp.int32, s.shape, 0) // rows_per_group
            cg = lax.broadcasted_iota(jnp.int32, s.shape, 1) // mem_per_group
            s = jnp.where(rg == cg, s, NEG_BIG)
        e = jnp.exp(s - jnp.max(s, axis=-1, keepdims=True))
        p = e / jnp.sum(e, axis=-1, keepdims=True)
        outs.append(_dot(p.astype(BF16), mv[:, sl]))
    o = jnp.concatenate(outs, axis=-1).astype(BF16)
    y = _dot(o, wo_ref[0]) + x
    o_ref[...] = y
    ms_y = jnp.mean(y * y, axis=-1, keepdims=True)
    on_ref[...] = ((y * lax.rsqrt(ms_y + RMS_EPS)) * g_next_ref[0]).astype(on_ref.dtype)


def _xattn(h, g, g_next, wq_bf, mk, mv, wo_bf, layer, *, heads, rows_per_seq, n_mem, grouped):
    m, d = h.shape
    xw = wq_bf.shape[2]
    scale = (xw // heads) ** -0.5
    if grouped:
        tm, mem_rows = m, mk.shape[1]
        mem_map = lambda i: (layer, 0, 0)
        rpg = rows_per_seq
    else:
        tm, mem_rows = _tile(rows_per_seq, 256), n_mem
        blocks_per_seq = rows_per_seq // tm
        mem_map = lambda i: (layer, i // blocks_per_seq, 0)
        rpg = None
    body = functools.partial(_xattn_body, heads=heads, scale=scale, rows_per_group=rpg, mem_per_group=n_mem)
    return pl.pallas_call(
        body,
        grid=(m // tm,),
        in_specs=[
            pl.BlockSpec((tm, d), lambda i: (i, 0)),
            pl.BlockSpec((1, 1, d), lambda i: (layer, 0, 0)),
            pl.BlockSpec((1, 1, d), lambda i: (layer, 0, 0)),
            pl.BlockSpec((1, d, xw), lambda i: (layer, 0, 0)),
            pl.BlockSpec((1, mem_rows, xw), mem_map),
            pl.BlockSpec((1, mem_rows, xw), mem_map),
            pl.BlockSpec((1, xw, d), lambda i: (layer, 0, 0)),
        ],
        out_specs=[pl.BlockSpec((tm, d), lambda i: (i, 0))] * 2,
        out_shape=[jax.ShapeDtypeStruct((m, d), F32), jax.ShapeDtypeStruct((m, d), BF16)],
        compiler_params=_params("parallel"),
        name="xattn",
    )(h, g, g_next, wq_bf, mk, mv, wo_bf)


def _swiglu_in_body(x_ref, xs_ref, wg_ref, wu_ref, o_ref, os_ref):
    wg = wg_ref[0].astype(BF16)
    wu = wu_ref[0].astype(BF16)

    def act(x):
        gate = _dot(x, wg)
        return (gate * _sigmoid(gate)) * _dot(x, wu)

    o_ref[...] = act(x_ref[...]).astype(o_ref.dtype)

    @pl.when(pl.program_id(0) == 0)
    def _():
        os_ref[...] = act(xs_ref[...]).astype(os_ref.dtype)


def _swiglu_in(xn, side, w_in, layer, *, tm_pref, tf_pref):
    m, d = xn.shape
    ms = side.shape[0]
    f = w_in.shape[2] // 2
    tm, tf = _tile(m, tm_pref), _tile(f, tf_pref)
    nf = f // tf
    return pl.pallas_call(
        _swiglu_in_body,
        grid=(m // tm, nf),
        in_specs=[
            pl.BlockSpec((tm, d), lambda i, j: (i, 0), pipeline_mode=pl.Buffered(1)),
            pl.BlockSpec((ms, d), lambda i, j: (0, 0)),
            pl.BlockSpec((1, d, tf), lambda i, j: (layer, 0, j)),
            pl.BlockSpec((1, d, tf), lambda i, j: (layer, 0, j + nf)),
        ],
        out_specs=[pl.BlockSpec((tm, tf), lambda i, j: (i, j)),
                   pl.BlockSpec((ms, tf), lambda i, j: (0, _side_col(i, j, nf - 1)))],
        out_shape=[jax.ShapeDtypeStruct((m, f), BF16), jax.ShapeDtypeStruct((ms, f), BF16)],
        compiler_params=_params("arbitrary", "arbitrary"),
        name="swiglu_in",
    )(xn, side, w_in, w_in)


def _gla_gate_body(x_ref, wg1_ref, wgate_ref, b_ref, o_ref):
    g1 = _dot(x_ref[...], wg1_ref[0])
    z = _dot(g1.astype(BF16), wgate_ref[0]) + b_ref[0]
    log_sig = jnp.minimum(z, 0.0) - jnp.log1p(jnp.exp(-jnp.abs(z)))
    o_ref[...] = log_sig / GLA_TAU


def _gla_gate(xn, wg1_bf, wgate_bf, b_gate, layer):
    m, d = xn.shape
    dk = wgate_bf.shape[2]
    tm = _tile(m, 512)
    return pl.pallas_call(
        _gla_gate_body,
        grid=(m // tm,),
        in_specs=[
            pl.BlockSpec((tm, d), lambda i: (i, 0)),
            pl.BlockSpec((1, d, GLA_GATE_PAD), lambda i: (layer, 0, 0)),
            pl.BlockSpec((1, GLA_GATE_PAD, dk), lambda i: (layer, 0, 0)),
            pl.BlockSpec((1, 1, dk), lambda i: (layer, 0, 0)),
        ],
        out_specs=pl.BlockSpec((tm, dk), lambda i: (i, 0)),
        out_shape=jax.ShapeDtypeStruct((m, dk), F32),
        compiler_params=_params("parallel"),
        name="gla_gate",
    )(xn, wg1_bf, wgate_bf, b_gate)


def _gla_body(q_ref, k_ref, v_ref, r_ref, la_ref, s0_ref, gh_ref, y_ref, sout_ref, s_scr, *, chunk, q_scale):
    t = pl.program_id(2)
    group, dkh, dvh = s_scr.shape
    tt = q_ref.shape[1]
    n = tt // chunk

    @pl.when(t == 0)
    def _():
        for g in range(group):
            s_scr[g] = s0_ref[0, 0, g]

    row = lax.broadcasted_iota(jnp.int32, (tt, tt), 0)
    col = lax.broadcasted_iota(jnp.int32, (tt, tt), 1)
    tri_sub = jnp.where((row >= col) & (row // chunk == col // chunk), 1.0, 0.0).astype(BF16)
    gh = gh_ref[0]
    ksl = [slice(g * dkh, (g + 1) * dkh) for g in range(group)]
    vsl = [slice(g * dvh, (g + 1) * dvh) for g in range(group)]
    rows = [slice(i * chunk, (i + 1) * chunk) for i in range(n)]

    def per_chunk(vecs):
        return jnp.concatenate([jnp.broadcast_to(v_, (chunk, v_.shape[1])) for v_ in vecs], axis=0)

    parts = _split3(la_ref[0])
    b = _dot(tri_sub, parts[0]) + _dot(tri_sub, parts[1]) + _dot(tri_sub, parts[2])
    total = [b[(i + 1) * chunk - 1:(i + 1) * chunk, :] for i in range(n)]
    start = [jnp.zeros_like(total[0])]
    for i in range(n):
        start.append(start[i] + total[i])

    qd = (q_ref[0] * q_scale) * jnp.exp(b)
    kf = k_ref[0]
    kd = kf * jnp.exp(per_chunk(total) - b)
    vb = v_ref[0].astype(BF16)
    q_dec = qd.astype(BF16)
    k_inv = (kf * jnp.exp(-b)).astype(BF16)
    q_state = (qd * per_chunk([jnp.exp(c_) for c_ in start[:n]])).astype(BF16)
    k_end = (kd * per_chunk([jnp.exp(start[n] - start[j + 1]) for j in range(n)])).astype(BF16)
    decay = jnp.exp(start[n])

    s = [s_scr[g] for g in range(group)]
    o_state = [_dot(q_state[:, ks], s_.astype(BF16)) for ks, s_ in zip(ksl, s)]
    upd = [_dot_tn(k_end[:, ks], vb[:, vs]) for ks, vs in zip(ksl, vsl)]
    for g in range(group):
        decay_col = jnp.transpose(jnp.broadcast_to(decay[:, ksl[g]], (LANES_V7X, dkh)))
        s_scr[g] = s[g] * jnp.tile(decay_col, (1, dvh // LANES_V7X)) + upd[g]

    keys = []
    for i in range(n):
        past = [(kd[rows[j], :] * jnp.exp(start[i] - start[j + 1])).astype(BF16) for j in range(i)]
        keys.append(jnp.concatenate(past + [k_inv[rows[i], :]], axis=0))
    scores = {}
    for i in range(n):
        width = (i + 1) * chunk
        visible = (lax.broadcasted_iota(jnp.int32, (chunk, width), 1)
                   <= lax.broadcasted_iota(jnp.int32, (chunk, width), 0) + i * chunk)
        for g in range(group):
            sc = _dot_nt(q_dec[rows[i], ksl[g]], keys[i][:, ksl[g]])
            scores[i, g] = jnp.where(visible, sc, 0.0).astype(BF16)
    intra = {(i, g): _dot(scores[i, g], vb[0:(i + 1) * chunk, vsl[g]]) for i in range(n) for g in range(group)}
    for g in range(group):
        o = o_state[g] + jnp.concatenate([intra[i, g] for i in range(n)], axis=0)
        ms = jnp.mean(o * o, axis=-1, keepdims=True)
        on = (o * lax.rsqrt(ms + RMS_EPS)) * gh
        rr = r_ref[0, :, vsl[g]]
        y_ref[0, :, vsl[g]] = (on * (rr * _sigmoid(rr))).astype(y_ref.dtype)

    @pl.when(t == pl.num_programs(2) - 1)
    def _():
        for g in range(group):
            sout_ref[0, g] = s_scr[g]


def _gla_recurrence(proj, log_a, s0, s0_layer, g_head, layer, *, heads, dk, dv):
    bsz, seq, _ = proj.shape
    dkh, dvh = dk // heads, dv // heads
    group = GLA_HEAD_GROUP
    tt = _tile(seq, 256)
    assert tt % GLA_CHUNK == 0 and heads % group == 0 and (2 * dk) % (group * dvh) == 0
    hg = heads // group
    v_off = 2 * dk // (group * dvh)
    body = functools.partial(_gla_body, chunk=GLA_CHUNK, q_scale=dkh ** -0.5)
    return pl.pallas_call(
        body,
        grid=(bsz, hg, seq // tt),
        in_specs=[
            pl.BlockSpec((1, tt, group * dkh), lambda b, h, t: (b, t, h)),
            pl.BlockSpec((1, tt, group * dkh), lambda b, h, t: (b, t, hg + h)),
            pl.BlockSpec((1, tt, group * dvh), lambda b, h, t: (b, t, v_off + h)),
            pl.BlockSpec((1, tt, group * dvh), lambda b, h, t: (b, t, v_off + hg + h)),
            pl.BlockSpec((1, tt, group * dkh), lambda b, h, t: (b, t, h)),
            pl.BlockSpec((1, 1, group, dkh, dvh), lambda b, h, t: (s0_layer, b, h, 0, 0)),
            pl.BlockSpec((1, 1, dvh), lambda b, h, t: (layer, 0, 0)),
        ],
        out_specs=[
            pl.BlockSpec((1, tt, group * dvh), lambda b, h, t: (b, t, h)),
            pl.BlockSpec((1, group, dkh, dvh), lambda b, h, t: (b, h, 0, 0)),
        ],
        out_shape=[
            jax.ShapeDtypeStruct((bsz, seq, dv), BF16),
            jax.ShapeDtypeStruct((bsz, heads, dkh, dvh), F32),
        ],
        scratch_shapes=[pltpu.VMEM((group, dkh, dvh), F32)],
        compiler_params=_params("parallel", "parallel", "arbitrary"),
        name="gla_recurrence",
    )(proj, proj, proj, proj, log_a, s0, g_head)


def _gla_mixer(xp, xs, hp, hs, state, layer, seq_p, seq_s, w_in, wg1_bf, wgate_bf, b_gate, g_head, w_out, *, heads, dk, dv):
    proj_p, proj_s = _matmul(xp, jnp.swapaxes(w_in, 1, 2), layer, n_out=2 * dk + 2 * dv, side=xs,
                             w_transposed=True, name="gla_in")

    def recur(xn, proj, seq, s0, s0_layer):
        bsz = xn.shape[0] // seq
        log_a = _gla_gate(xn, wg1_bf, wgate_bf, b_gate, layer).reshape(bsz, seq, dk)
        proj = proj.reshape(bsz, seq, -1)
        pad = (-seq) % GLA_CHUNK
        if pad:
            proj = jnp.pad(proj, ((0, 0), (0, pad), (0, 0)))
            log_a = jnp.pad(log_a, ((0, 0), (0, pad), (0, 0)))
        y, s_new = _gla_recurrence(proj, log_a, s0, s0_layer, g_head, layer, heads=heads, dk=dk, dv=dv)
        return y[:, :seq].reshape(xn.shape[0], dv), s_new

    zero_state = jnp.zeros((1, xp.shape[0] // seq_p) + state.shape[2:], F32)
    yp, sp = recur(xp, proj_p, seq_p, zero_state, 0)
    ys, ss = recur(xs, proj_s, seq_s, state, layer)
    hp, hs = _matmul(yp, w_out, layer, n_out=w_out.shape[2], res=hp, side=ys, side_res=hs, name="gla_out")
    return hp, hs, sp, ss


def _sb_prompt_body(bias_ref, q_ref, k_ref, v_ref, o_ref, kb_scr, vb_scr, *, scale):
    h = pl.program_id(1)
    i = pl.program_id(2)
    blk, rows = SB_BLOCK, SB_ROWS
    n_sub = blk // rows
    hd = q_ref.shape[2]

    @pl.when(i == 0)
    def _():
        kb_scr[...] = k_ref[0].astype(BF16)
        vb_scr[...] = v_ref[0].astype(BF16)

    bias = bias_ref[h]
    r_io = lax.broadcasted_iota(jnp.int32, (blk, blk), 0)
    c_io = lax.broadcasted_iota(jnp.int32, (blk, blk), 1)
    upper = jnp.where(r_io > c_io, -1.0, 0.0).astype(BF16)
    upper2 = jnp.concatenate([upper, upper], axis=0)
    sub_r = lax.broadcasted_iota(jnp.int32, (rows, blk), 0)
    sub_c = lax.broadcasted_iota(jnp.int32, (rows, blk), 1)

    def sweep(jobs, state):
        state = dict(state)
        keys = [pl.ds(pl.multiple_of(kb * blk, blk), blk) for _, kb, _ in jobs]
        z = [_dot_nt(q_ref[0, r0:r0 + rows, :], kb_scr[ks, :]) * scale + bias for (r0, _, _), ks in zip(jobs, keys)]
        sp = [_softplus(zz) for zz in z]
        vis = [sub_c < sub_r + (r0 % blk) if m else None for r0, _, m in jobs]
        nlk = [s_ if v_ is None else jnp.where(v_, s_, 0.0) for s_, v_ in zip(sp, vis)]
        later = [_dot(jnp.concatenate(_split2(l_), axis=1), upper2) for l_ in nlk]
        for (r0, _, _), ks, zz, s_, v_, l_, lt in zip(jobs, keys, z, sp, vis, nlk, later):
            tail, o = state[r0]
            a = jnp.exp((jnp.concatenate([tail] * (blk // LANES_V7X), axis=1) + lt) + (zz - s_))
            if v_ is not None:
                a = jnp.where(v_, a, 0.0)
            o = o + _dot(a.astype(BF16), vb_scr[ks, :])
            tail = tail + jnp.broadcast_to(lt[:, 0:1] - l_[:, 0:1], tail.shape)
            state[r0] = (tail, o)
        return state

    zero = (jnp.zeros((rows, LANES_V7X), F32), jnp.zeros((rows, hd), F32))
    rows_a = [s * rows for s in range(n_sub)]
    rows_b = [blk + s * rows for s in range(n_sub)]
    all_rows = rows_a + rows_b
    state = sweep([(r, 2 * i + 1, True) for r in rows_b] + [(r, 2 * i, True) for r in rows_a]
                  + [(r, 2 * i, False) for r in rows_b], {r: zero for r in all_rows})

    def step(jj, carry):
        kb = 2 * i - 1 - jj
        state = sweep([(r, kb, False) for r in all_rows], dict(zip(all_rows, carry)))
        return tuple(state[r] for r in all_rows)

    final = lax.fori_loop(0, 2 * i, step, tuple(state[r] for r in all_rows))
    for r, (_, o) in zip(all_rows, final):
        o_ref[0, r:r + rows, :] = o.astype(o_ref.dtype)


def _sb_prompt(q, k, v, bias, *, heads):
    bsz, seq, d = q.shape
    hd = d // heads
    tq = 2 * SB_BLOCK
    assert hd == LANES_V7X and seq % tq == 0
    body = functools.partial(_sb_prompt_body, scale=hd ** -0.5)
    return pl.pallas_call(
        body,
        grid=(bsz, heads, seq // tq),
        in_specs=[
            pl.BlockSpec(memory_space=pltpu.SMEM),
            pl.BlockSpec((1, tq, hd), lambda b, h, i: (b, i, h)),
            pl.BlockSpec((1, seq, hd), lambda b, h, i: (b, 0, h)),
            pl.BlockSpec((1, seq, hd), lambda b, h, i: (b, 0, h)),
        ],
        out_specs=pl.BlockSpec((1, tq, hd), lambda b, h, i: (b, i, h)),
        out_shape=jax.ShapeDtypeStruct((bsz, seq, d), BF16),
        scratch_shapes=[pltpu.VMEM((seq, hd), BF16), pltpu.VMEM((seq, hd), BF16)],
        compiler_params=_params("parallel", "parallel", "arbitrary"),
        name="sb_prompt",
    )(bias, q, k, v)


def _sb_sample_body(pt_ref, qbd_ref, bias_ref, kn_ref, vn_ref, kp_ref, vp_ref, o_ref, acc_scr, tail_scr,
                    *, heads, n_new, scale):
    del pt_ref
    p = pl.program_id(1)
    page, nl = kn_ref.shape[1], qbd_ref.shape[2]
    lower = (lax.broadcasted_iota(jnp.int32, (page, page), 1) > lax.broadcasted_iota(jnp.int32, (page, page), 0)).astype(BF16)

    def block(kb, vb, vis):
        z = _dot(kb, qbd_ref[0]) * scale + bias_ref[...]
        sp = _softplus(z)
        lk = -sp if vis is None else jnp.where(vis, -sp, 0.0)
        lk_hi, lk_mid, lk_lo = _split3(lk)
        later = _dot(lower, lk_hi) + _dot(lower, lk_mid) + _dot(lower, lk_lo)
        a = jnp.exp((tail_scr[0:1, :] + later) + (z - sp))
        if vis is not None:
            a = jnp.where(vis, a, 0.0)
        acc_scr[...] += _dot(jnp.transpose(a).astype(BF16), vb)
        tail_scr[0:1, :] = tail_scr[0:1, :] + jnp.sum(lk, axis=0, keepdims=True)

    def heads_to_lanes(ref):
        by_head = jnp.swapaxes(ref[0, 0], 0, 1)
        return jnp.concatenate([by_head[h].astype(BF16) for h in range(heads)], axis=1)

    @pl.when(p == 0)
    def _():
        acc_scr[...] = jnp.zeros_like(acc_scr)
        tail_scr[...] = jnp.zeros_like(tail_scr)
        key = lax.broadcasted_iota(jnp.int32, (page, nl), 0)
        tok = lax.broadcasted_iota(jnp.int32, (page, nl), 1) // heads
        block(kn_ref[0].astype(BF16), vn_ref[0].astype(BF16), (key < tok) & (key < n_new))

    @pl.when(p > 0)
    def _():
        block(heads_to_lanes(kp_ref), heads_to_lanes(vp_ref), None)

    @pl.when(p == pl.num_programs(1) - 1)
    def _():
        d = acc_scr.shape[1]
        hd = d // heads
        own = lax.broadcasted_iota(jnp.int32, (heads, d), 0) == lax.broadcasted_iota(jnp.int32, (heads, d), 1) // hd
        for t in range(n_new):
            rows = acc_scr[t * heads:(t + 1) * heads, :]
            o_ref[0, t:t + 1, :] = jnp.sum(jnp.where(own, rows, 0.0), axis=0, keepdims=True).astype(o_ref.dtype)


def _sb_sample(q, k_new, v_new, cache_k, cache_v, layer, page_table, bias, *, heads):
    bd, t_new, d = q.shape
    hd = d // heads
    page = cache_k.shape[2]
    n_pages = page_table.shape[1]
    nl = LANES_V7X
    assert heads * t_new <= nl and heads % 8 == 0 and hd % LANES_V7X == 0 and t_new <= page
    q4 = q.reshape(bd, t_new, heads, hd)
    qbd = jnp.einsum("bthx,hg->bhxtg", q4, jnp.eye(heads, dtype=q.dtype)).reshape(bd, d, t_new * heads)
    qbd = jnp.pad(qbd, ((0, 0), (0, 0), (0, nl - t_new * heads))).astype(BF16)
    bias_l = jnp.pad(jnp.tile(bias, t_new), (0, nl - t_new * heads)).reshape(1, nl)
    kn = jnp.pad(k_new, ((0, 0), (0, page - t_new), (0, 0)))
    vn = jnp.pad(v_new, ((0, 0), (0, page - t_new), (0, 0)))

    def page_map(b, p, pt):
        return (layer, pt[b, n_pages - jnp.maximum(p, 1)], 0, 0, 0)

    body = functools.partial(_sb_sample_body, heads=heads, n_new=t_new, scale=hd ** -0.5)
    grid_spec = pltpu.PrefetchScalarGridSpec(
        num_scalar_prefetch=1,
        grid=(bd, n_pages + 1),
        in_specs=[
            pl.BlockSpec((1, d, nl), lambda b, p, pt: (b, 0, 0)),
            pl.BlockSpec((1, nl), lambda b, p, pt: (0, 0)),
            pl.BlockSpec((1, page, d), lambda b, p, pt: (b, 0, 0)),
            pl.BlockSpec((1, page, d), lambda b, p, pt: (b, 0, 0)),
            pl.BlockSpec((1, 1, page, heads, hd), page_map),
            pl.BlockSpec((1, 1, page, heads, hd), page_map),
        ],
        out_specs=pl.BlockSpec((1, t_new, d), lambda b, p, pt: (b, 0, 0)),
        scratch_shapes=[pltpu.VMEM((nl, d), F32), pltpu.VMEM((8, nl), F32)],
    )
    return pl.pallas_call(
        body,
        grid_spec=grid_spec,
        out_shape=jax.ShapeDtypeStruct((bd, t_new, d), F32),
        compiler_params=_params("parallel", "arbitrary"),
        name="sb_sample",
    )(page_table, qbd, bias_l, kn, vn, cache_k, cache_v)


def _conv_prompt_body(x_ref, wb_ref, wc_ref, wh_ref, wconv_ref, buf0_ref, y_ref, st_ref, u_scr, *, blocks_per_seq):
    m = pl.program_id(1)
    tm = x_ref.shape[0]
    x = x_ref[...]
    bg = _dot(x, wb_ref[0].astype(BF16))
    u = _dot(x, wc_ref[0].astype(BF16)) * _dot(x, wh_ref[0].astype(BF16))

    @pl.when(m % blocks_per_seq == 0)
    def _():
        u_scr[6:8, :] = buf0_ref[0]

    u_scr[8:8 + tm, :] = u
    wconv = wconv_ref[0]
    conv = wconv[0:1, :] * u_scr[6:6 + tm, :] + wconv[1:2, :] * u_scr[7:7 + tm, :] + wconv[2:3, :] * u
    y_ref[...] = (bg * conv).astype(y_ref.dtype)
    st_ref[0] = u_scr[tm + 6:tm + 8, :]
    u_scr[0:8, :] = u_scr[tm:tm + 8, :]


def _conv_prompt(xn, w_in, w_conv, layer, buf0, seq):
    m, d = xn.shape
    tm = _tile(seq, 1024)
    tn = _tile(d, 256)
    nb = d // tn
    blocks_per_seq = seq // tm
    body = functools.partial(_conv_prompt_body, blocks_per_seq=blocks_per_seq)
    return pl.pallas_call(
        body,
        grid=(nb, m // tm),
        in_specs=[
            pl.BlockSpec((tm, d), lambda n, i: (i, 0)),
            pl.BlockSpec((1, d, tn), lambda n, i: (layer, 0, n)),
            pl.BlockSpec((1, d, tn), lambda n, i: (layer, 0, nb + n)),
            pl.BlockSpec((1, d, tn), lambda n, i: (layer, 0, 2 * nb + n)),
            pl.BlockSpec((1, w_conv.shape[1], tn), lambda n, i: (layer, 0, n)),
            pl.BlockSpec((1, 2, tn), lambda n, i: (i // blocks_per_seq, 0, n)),
        ],
        out_specs=[
            pl.BlockSpec((tm, tn), lambda n, i: (i, n)),
            pl.BlockSpec((1, 2, tn), lambda n, i: (i // blocks_per_seq, 0, n)),
        ],
        out_shape=[
            jax.ShapeDtypeStruct((m, d), BF16),
            jax.ShapeDtypeStruct((m // seq, 2, d), F32),
        ],
        scratch_shapes=[pltpu.VMEM((tm + 8, tn), F32)],
        compiler_params=_params("parallel", "arbitrary"),
        name="conv_prompt",
    )(xn, w_in, w_in, w_in, w_conv, buf0)


def _conv_sample_body(proj_ref, buf0_ref, wconv_ref, y_ref, st_ref, *, seq):
    d = y_ref.shape[2]
    taps = [buf0_ref[0], buf0_ref[1]]
    for t in range(seq):
        taps.append(proj_ref[t, :, d:2 * d] * proj_ref[t, :, 2 * d:3 * d])
    for t in range(seq):
        conv = wconv_ref[0:1, :] * taps[t] + wconv_ref[1:2, :] * taps[t + 1] + wconv_ref[2:3, :] * taps[t + 2]
        y_ref[t] = (proj_ref[t, :, 0:d] * conv).astype(y_ref.dtype)
    st_ref[0] = taps[seq]
    st_ref[1] = taps[seq + 1]


def _conv_sample(proj_tm, buf0_tm, w_conv):
    seq, bd, d3 = proj_tm.shape
    d = d3 // 3
    return pl.pallas_call(
        functools.partial(_conv_sample_body, seq=seq),
        out_shape=[jax.ShapeDtypeStruct((seq, bd, d), BF16), jax.ShapeDtypeStruct((2, bd, d), F32)],
        compiler_params=pltpu.CompilerParams(vmem_limit_bytes=VMEM_LIMIT_V7X),
        name="conv_sample",
    )(proj_tm, buf0_tm, w_conv)


def kernel(x_prompt, x_sample, state_gla, cache_sb_k, cache_sb_v, state_conv, cache_mem_k, cache_mem_v, page_table, mem_prompt, g_mix, g_xattn, g_mem, g_ffn, g_final, w_gla_in, w_gla_gate, b_gla_gate, g_gla_head, w_gla_out, w_sb_in, b_sb, w_sb_out, w_conv_in, w_conv, w_conv_out, w_xq, w_xkv, w_xo, w_ffn_in, w_ffn_out):
    bsz, seq, d = x_prompt.shape
    bd, t_new, _ = x_sample.shape
    depth = g_mix.shape[0]
    n_mem = mem_prompt.shape[1]
    gla_heads = state_gla.shape[2]
    dk = w_gla_gate.shape[2]
    dv = w_gla_out.shape[1]
    sb_heads = b_sb.shape[1]
    x_heads = cache_mem_k.shape[3]
    xw = w_xq.shape[2]
    mp, ms = bsz * seq, bd * t_new

    bf = lambda w: w.astype(BF16)
    rank = w_gla_gate.shape[1]
    w_gla_g1 = bf(jnp.pad(w_gla_in[:, :, 2 * dk + 2 * dv:], ((0, 0), (0, 0), (0, GLA_GATE_PAD - rank))))
    w_gla_gate_p = bf(jnp.pad(w_gla_gate, ((0, 0), (0, GLA_GATE_PAD - rank), (0, 0))))
    w_xq_bf, w_xkv_bf, w_xo_bf, w_ffn_out_bf = bf(w_xq), bf(w_xkv), bf(w_xo), bf(w_ffn_out)
    b_gla_gate3 = b_gla_gate.reshape(-1, 1, dk)
    g_gla_head3 = g_gla_head.reshape(-1, 1, dv // gla_heads)
    g_xattn3 = g_xattn.reshape(depth, 1, d)
    g_ffn3 = g_ffn.reshape(depth, 1, d)
    cache_mem_k3 = cache_mem_k.reshape(depth, bd * n_mem, xw)
    cache_mem_v3 = cache_mem_v.reshape(depth, bd * n_mem, xw)

    hp = x_prompt.reshape(mp, d)
    hs = x_sample.reshape(ms, d)
    mem_k, mem_v = _mem_kv(mem_prompt.reshape(bsz * n_mem, d), g_mem, w_xkv_bf)

    gla_p, gla_s, sbk_p, sbv_p, sbk_s, sbv_s, conv_p, conv_s = [], [], [], [], [], [], [], []
    for i in range(depth):
        kind, j = i % 3, i // 3
        xp = _rmsnorm(hp, g_mix[i], BF16)
        xs = _rmsnorm(hs, g_mix[i], BF16)
        if kind == 0:
            hp, hs, sp, ss = _gla_mixer(xp, xs, hp, hs, state_gla, j, seq, t_new, w_gla_in, w_gla_g1, w_gla_gate_p,
                                        b_gla_gate3, g_gla_head3, w_gla_out, heads=gla_heads, dk=dk, dv=dv)
            gla_p.append(sp)
            gla_s.append(ss)
        elif kind == 1:
            qp, qs = _matmul(xp, w_sb_in, j, n_out=d, col_off=0, out_dtype=BF16, side=xs, name="sb_q")
            kp, kn = _matmul(xp, w_sb_in, j, n_out=d, col_off=d, side=xs, name="sb_k")
            vp, vn = _matmul(xp, w_sb_in, j, n_out=d, col_off=2 * d, side=xs, name="sb_v")
            qs, kn, vn = (a.reshape(bd, t_new, d) for a in (qs, kn, vn))
            op = _sb_prompt(qp.reshape(bsz, seq, d), kp.reshape(bsz, seq, d), vp.reshape(bsz, seq, d), b_sb[j],
                            heads=sb_heads)
            os_ = _sb_sample(qs, kn, vn, cache_sb_k, cache_sb_v, j, page_table, b_sb[j], heads=sb_heads)
            hp, hs = _matmul(op.reshape(mp, d), w_sb_out, j, n_out=d, res=hp,
                             side=os_.reshape(ms, d).astype(BF16), side_res=hs, name="sb_out")
            sbk_p.append(kp)
            sbv_p.append(vp)
            sbk_s.append(kn)
            sbv_s.append(vn)
        else:
            yp, bp = _conv_prompt(xp, w_conv_in, w_conv, j, jnp.zeros((bsz, 2, d), F32), seq)
            proj_s = _matmul(xs, w_conv_in, j, n_out=3 * d, name="conv_in_sample")
            ys, bs = _conv_sample(proj_s.reshape(bd, t_new, 3 * d).transpose(1, 0, 2),
                                  state_conv[j].transpose(1, 0, 2), w_conv[j])
            hp, hs = _matmul(yp, w_conv_out, j, n_out=d, res=hp,
                             side=ys.transpose(1, 0, 2).reshape(ms, d), side_res=hs, name="conv_out")
            conv_p.append(bp)
            conv_s.append(bs.transpose(1, 0, 2))
        hp, xp_ffn = _xattn(hp, g_xattn3, g_ffn3, w_xq_bf, mem_k, mem_v, w_xo_bf, i,
                            heads=x_heads, rows_per_seq=seq, n_mem=n_mem, grouped=False)
        hs, xs_ffn = _xattn(hs, g_xattn3, g_ffn3, w_xq_bf, cache_mem_k3, cache_mem_v3, w_xo_bf, i,
                            heads=x_heads, rows_per_seq=t_new, n_mem=n_mem, grouped=True)
        ap, as_ = _swiglu_in(xp_ffn, xs_ffn, w_ffn_in, i, tm_pref=2048, tf_pref=256)
        hp, hs = _matmul(ap, w_ffn_out_bf, i, n_out=d, res=hp, side=as_, side_res=hs,
                         tm_pref=1024, tn_pref=256, single_buffer_x=True, name="ffn_out")

    y_prompt = _rmsnorm(hp, g_final, F32).reshape(bsz, seq, d)
    y_sample = _rmsnorm(hs, g_final, F32).reshape(bd, t_new, d)
    hd = d // sb_heads
    sb_p = lambda xs_: jnp.stack(xs_).reshape(-1, bsz, seq, sb_heads, hd)
    sb_s = lambda xs_: jnp.stack(xs_).reshape(-1, bd, t_new, sb_heads, hd)
    xhd = xw // x_heads
    return (y_prompt, y_sample,
            jnp.stack(gla_p), jnp.stack(gla_s),
            sb_p(sbk_p), sb_p(sbv_p), sb_s(sbk_s), sb_s(sbv_s),
            jnp.stack(conv_p), jnp.stack(conv_s),
            mem_k.reshape(depth, bsz, n_mem, x_heads, xhd), mem_v.reshape(depth, bsz, n_mem, x_heads, xhd))
```

```python
import functools

import jax
import jax.numpy as jnp
from jax import lax
from jax.experimental import pallas as pl
from jax.experimental.pallas import tpu as pltpu

F32 = jnp.float32
BF16 = jnp.bfloat16

LANES_V7X = 128
VMEM_LIMIT_V7X = 56 * 1024 * 1024
RMS_EPS = 1e-6
GLA_TAU = 16.0
GLA_CHUNK = 32
GLA_GATE_PAD = LANES_V7X
GLA_HEAD_GROUP = 2
SB_BLOCK = 256
SB_ROWS = 128
SB_PAGES_PER_STEP = 4
NEG_BIG = -1e30


def _params(*sem):
    return pltpu.CompilerParams(dimension_semantics=sem, vmem_limit_bytes=VMEM_LIMIT_V7X)


def _tile(n, pref):
    if n <= pref:
        return n
    t = (pref // LANES_V7X) * LANES_V7X
    while t > LANES_V7X and n % t:
        t -= LANES_V7X
    assert n % t == 0, (n, pref)
    return t


def _softplus(z):
    return jnp.maximum(z, 0.0) + jnp.log(1.0 + jnp.exp(-jnp.abs(z)))


def _sigmoid(z):
    return 1.0 / (1.0 + jnp.exp(-z))


def _split2(x):
    hi = x.astype(BF16)
    lo = (x - hi.astype(F32)).astype(BF16)
    return hi, lo


def _split3(x):
    hi = x.astype(BF16)
    r1 = x - hi.astype(F32)
    mid = r1.astype(BF16)
    lo = (r1 - mid.astype(F32)).astype(BF16)
    return hi, mid, lo


def _dot(a, b):
    return jnp.dot(a, b, preferred_element_type=F32)


def _dot_nt(a, b):
    return lax.dot_general(a, b, (((1,), (1,)), ((), ())), preferred_element_type=F32)


def _dot_tn(a, b):
    return lax.dot_general(a, b, (((0,), (0,)), ((), ())), preferred_element_type=F32)


def _rmsnorm_body(x_ref, g_ref, o_ref):
    x = x_ref[...]
    ms = jnp.mean(x * x, axis=-1, keepdims=True)
    o_ref[...] = ((x * lax.rsqrt(ms + RMS_EPS)) * g_ref[...]).astype(o_ref.dtype)


def _rmsnorm(x, g, out_dtype):
    m, d = x.shape
    tm = _tile(m, 512)
    return pl.pallas_call(
        _rmsnorm_body,
        grid=(m // tm,),
        in_specs=[pl.BlockSpec((tm, d), lambda i: (i, 0)), pl.BlockSpec((1, d), lambda i: (0, 0))],
        out_specs=pl.BlockSpec((tm, d), lambda i: (i, 0)),
        out_shape=jax.ShapeDtypeStruct((m, d), out_dtype),
        compiler_params=_params("parallel"),
        name="rmsnorm",
    )(x, g.reshape(1, d))


def _mm_body(*refs, has_res, has_side, has_side_res, w_transposed, row_axis):
    dot = _dot_nt if w_transposed else _dot
    it = iter(refs)
    x_ref, w_ref = next(it), next(it)
    r_ref = next(it) if has_res else None
    xs_ref = next(it) if has_side else None
    rs_ref = next(it) if has_side_res else None
    o_ref = next(it)
    os_ref = next(it) if has_side else None
    wb = w_ref[0].astype(BF16)
    acc = dot(x_ref[...], wb)
    if has_res:
        acc = acc + r_ref[...]
    o_ref[...] = acc.astype(o_ref.dtype)
    if has_side:
        @pl.when(pl.program_id(row_axis) == 0)
        def _():
            side = dot(xs_ref[...], wb)
            if has_side_res:
                side = side + rs_ref[...]
            os_ref[...] = side.astype(os_ref.dtype)


def _side_col(i, j, last):
    return jnp.where(i == 0, j, last)


def _matmul(x, w, layer, *, n_out, col_off=0, res=None, out_dtype=F32, side=None, side_res=None, side_dtype=F32,
            tm_pref=1024, tn_pref=512, single_buffer_x=False, w_transposed=False, cols_outer=False, name="matmul"):
    m, k = x.shape
    tm, tn = _tile(m, tm_pref), _tile(n_out, tn_pref)
    assert col_off % tn == 0
    off = col_off // tn
    nj = n_out // tn
    if cols_outer:
        grid = (nj, m // tm)
        ij = lambda a, b: (b, a)
        side_col = lambda a, b: a
        w_mode = dict(pipeline_mode=pl.Buffered(1))
    else:
        grid = (m // tm, nj)
        ij = lambda a, b: (a, b)
        side_col = lambda a, b: _side_col(a, b, nj - 1)
        w_mode = {}
    x_mode = dict(pipeline_mode=pl.Buffered(1)) if single_buffer_x else {}
    w_spec = (pl.BlockSpec((1, tn, k), lambda a, b: (layer, ij(a, b)[1] + off, 0), **w_mode) if w_transposed else
              pl.BlockSpec((1, k, tn), lambda a, b: (layer, 0, ij(a, b)[1] + off), **w_mode))
    in_specs = [pl.BlockSpec((tm, k), lambda a, b: (ij(a, b)[0], 0), **x_mode), w_spec]
    args = [x, w]
    out_specs = [pl.BlockSpec((tm, tn), lambda a, b: ij(a, b))]
    out_shape = [jax.ShapeDtypeStruct((m, n_out), out_dtype)]
    if res is not None:
        in_specs.append(pl.BlockSpec((tm, tn), lambda a, b: ij(a, b)))
        args.append(res)
    if side is not None:
        ms = side.shape[0]
        in_specs.append(pl.BlockSpec((ms, k), lambda a, b: (0, 0)))
        args.append(side)
        if side_res is not None:
            in_specs.append(pl.BlockSpec((ms, tn), lambda a, b: (0, side_col(a, b))))
            args.append(side_res)
        out_specs.append(pl.BlockSpec((ms, tn), lambda a, b: (0, side_col(a, b))))
        out_shape.append(jax.ShapeDtypeStruct((ms, n_out), side_dtype))
    body = functools.partial(_mm_body, has_res=res is not None, has_side=side is not None,
                             has_side_res=side_res is not None, w_transposed=w_transposed,
                             row_axis=1 if cols_outer else 0)
    outs = pl.pallas_call(
        body,
        grid=grid,
        in_specs=in_specs,
        out_specs=out_specs,
        out_shape=out_shape,
        compiler_params=_params("arbitrary", "arbitrary"),
        name=name,
    )(*args)
    return outs if side is not None else outs[0]


def _mem_kv_body(x_ref, g_ref, w_ref, k_ref, v_ref):
    x = x_ref[...]
    ms = jnp.mean(x * x, axis=-1, keepdims=True)
    xn = ((x * lax.rsqrt(ms + RMS_EPS)) * g_ref[0]).astype(BF16)
    kv = _dot(xn, w_ref[0])
    half = kv.shape[1] // 2
    k_ref[0] = kv[:, :half]
    v_ref[0] = kv[:, half:]


def _mem_kv(mem, g_mem, w_xkv_bf):
    rows, d = mem.shape
    depth, _, two_w = w_xkv_bf.shape
    xw = two_w // 2
    tm = _tile(rows, 512)
    out = jax.ShapeDtypeStruct((depth, rows, xw), F32)
    return pl.pallas_call(
        _mem_kv_body,
        grid=(depth, rows // tm),
        in_specs=[
            pl.BlockSpec((tm, d), lambda i, m: (m, 0)),
            pl.BlockSpec((1, 1, d), lambda i, m: (i, 0, 0)),
            pl.BlockSpec((1, d, two_w), lambda i, m: (i, 0, 0)),
        ],
        out_specs=[pl.BlockSpec((1, tm, xw), lambda i, m: (i, m, 0))] * 2,
        out_shape=[out, out],
        compiler_params=_params("parallel", "parallel"),
        name="mem_kv",
    )(mem, g_mem.reshape(depth, 1, d), w_xkv_bf)


def _xattn_body(x_ref, g_ref, g_next_ref, wq_ref, mk_ref, mv_ref, wo_ref, o_ref, on_ref, *, heads, scale,
                rows_per_group, mem_per_group):
    x = x_ref[...]
    ms = jnp.mean(x * x, axis=-1, keepdims=True)
    xn = ((x * lax.rsqrt(ms + RMS_EPS)) * g_ref[0]).astype(BF16)
    q = _dot(xn, wq_ref[0])
    mk = mk_ref[0].astype(BF16)
    mv = mv_ref[0].astype(BF16)
    hd = q.shape[1] // heads
    outs = []
    for h in range(heads):
        sl = slice(h * hd, (h + 1) * hd)
        s = _dot_nt(q[:, sl].astype(BF16), mk[:, sl]) * scale
        if rows_per_group is not None:
            rg = lax.broadcasted_iota(jnp.int32, s.shape, 0) // rows_per_group
            cg = lax.broadcasted_iota(jnp.int32, s.shape, 1) // mem_per_group
            s = jnp.where(rg == cg, s, NEG_BIG)
        e = jnp.exp(s - jnp.max(s, axis=-1, keepdims=True))
        p = e / jnp.sum(e, axis=-1, keepdims=True)
        outs.append(_dot(p.astype(BF16), mv[:, sl]))
    o = jnp.concatenate(outs, axis=-1).astype(BF16)
    y = _dot(o, wo_ref[0]) + x
    o_ref[...] = y
    ms_y = jnp.mean(y * y, axis=-1, keepdims=True)
    on_ref[...] = ((y * lax.rsqrt(ms_y + RMS_EPS)) * g_next_ref[0]).astype(on_ref.dtype)


def _xattn(h, g, g_next, wq_bf, mk, mv, wo_bf, layer, *, heads, rows_per_seq, n_mem, grouped):
    m, d = h.shape
    xw = wq_bf.shape[2]
    scale = (xw // heads) ** -0.5
    if grouped:
        tm, mem_rows = m, mk.shape[1]
        mem_map = lambda i: (layer, 0, 0)
        rpg = rows_per_seq
    else:
        tm, mem_rows = _tile(rows_per_seq, 256), n_mem
        blocks_per_seq = rows_per_seq // tm
        mem_map = lambda i: (layer, i // blocks_per_seq, 0)
        rpg = None
    body = functools.partial(_xattn_body, heads=heads, scale=scale, rows_per_group=rpg, mem_per_group=n_mem)
    return pl.pallas_call(
        body,
        grid=(m // tm,),
        in_specs=[
            pl.BlockSpec((tm, d), lambda i: (i, 0)),
            pl.BlockSpec((1, 1, d), lambda i: (layer, 0, 0)),
            pl.BlockSpec((1, 1, d), lambda i: (layer, 0, 0)),
            pl.BlockSpec((1, d, xw), lambda i: (layer, 0, 0)),
            pl.BlockSpec((1, mem_rows, xw), mem_map),
            pl.BlockSpec((1, mem_rows, xw), mem_map),
            pl.BlockSpec((1, xw, d), lambda i: (layer, 0, 0)),
        ],
        out_specs=[pl.BlockSpec((tm, d), lambda i: (i, 0))] * 2,
        out_shape=[jax.ShapeDtypeStruct((m, d), F32), jax.ShapeDtypeStruct((m, d), BF16)],
        compiler_params=_params("parallel"),
        name="xattn",
    )(h, g, g_next, wq_bf, mk, mv, wo_bf)


def _swiglu_in_body(x_ref, xs_ref, wg_ref, wu_ref, wnext_ref, o_ref, os_ref, wnext_bf_ref):
    wg = wg_ref[0].astype(BF16)
    wu = wu_ref[0].astype(BF16)

    def act(x):
        gate = _dot(x, wg)
        return (gate * _sigmoid(gate)) * _dot(x, wu)

    o_ref[...] = act(x_ref[...]).astype(o_ref.dtype)

    @pl.when(pl.program_id(0) == 0)
    def _():
        os_ref[...] = act(xs_ref[...]).astype(os_ref.dtype)
        wnext_bf_ref[...] = wnext_ref[0].astype(BF16)


def _swiglu_in(xn, side, w_in, w_next, layer, *, tm_pref, tf_pref):
    m, d = xn.shape
    ms = side.shape[0]
    f = w_in.shape[2] // 2
    tm, tf = _tile(m, tm_pref), _tile(f, tf_pref)
    nf = f // tf
    d_next = w_next.shape[2]
    return pl.pallas_call(
        _swiglu_in_body,
        grid=(m // tm, nf),
        in_specs=[
            pl.BlockSpec((tm, d), lambda i, j: (i, 0), pipeline_mode=pl.Buffered(1)),
            pl.BlockSpec((ms, d), lambda i, j: (0, 0)),
            pl.BlockSpec((1, d, tf), lambda i, j: (layer, 0, j)),
            pl.BlockSpec((1, d, tf), lambda i, j: (layer, 0, j + nf)),
            pl.BlockSpec((1, tf, d_next), lambda i, j: (layer, _side_col(i, j, nf - 1), 0)),
        ],
        out_specs=[pl.BlockSpec((tm, tf), lambda i, j: (i, j)),
                   pl.BlockSpec((ms, tf), lambda i, j: (0, _side_col(i, j, nf - 1))),
                   pl.BlockSpec((tf, d_next), lambda i, j: (_side_col(i, j, nf - 1), 0))],
        out_shape=[jax.ShapeDtypeStruct((m, f), BF16), jax.ShapeDtypeStruct((ms, f), BF16),
                   jax.ShapeDtypeStruct((f, d_next), BF16)],
        compiler_params=_params("arbitrary", "arbitrary"),
        name="swiglu_in",
    )(xn, side, w_in, w_in, w_next)


def _gla_gate_body(x_ref, wg1_ref, wgate_ref, b_ref, o_ref):
    g1 = _dot(x_ref[...], wg1_ref[0])
    z = _dot(g1.astype(BF16), wgate_ref[0]) + b_ref[0]
    log_sig = jnp.minimum(z, 0.0) - jnp.log1p(jnp.exp(-jnp.abs(z)))
    o_ref[...] = log_sig / GLA_TAU


def _gla_gate(xn, wg1_bf, wgate_bf, b_gate, layer):
    m, d = xn.shape
    dk = wgate_bf.shape[2]
    tm = _tile(m, 512)
    return pl.pallas_call(
        _gla_gate_body,
        grid=(m // tm,),
        in_specs=[
            pl.BlockSpec((tm, d), lambda i: (i, 0)),
            pl.BlockSpec((1, d, GLA_GATE_PAD), lambda i: (layer, 0, 0)),
            pl.BlockSpec((1, GLA_GATE_PAD, dk), lambda i: (layer, 0, 0)),
            pl.BlockSpec((1, 1, dk), lambda i: (layer, 0, 0)),
        ],
        out_specs=pl.BlockSpec((tm, dk), lambda i: (i, 0)),
        out_shape=jax.ShapeDtypeStruct((m, dk), F32),
        compiler_params=_params("parallel"),
        name="gla_gate",
    )(xn, wg1_bf, wgate_bf, b_gate)


def _gla_body(q_ref, k_ref, v_ref, r_ref, la_ref, s0_ref, gh_ref, *refs, chunk, q_scale):
    y_ref, sout_ref, s_scr = refs[-3:]
    t = pl.program_id(2)
    group, dkh, dvh = s_scr.shape
    tt = q_ref.shape[1]
    n = tt // chunk

    @pl.when(t == 0)
    def _():
        for g in range(group):
            s_scr[g] = s0_ref[0, 0, g]

    row = lax.broadcasted_iota(jnp.int32, (tt, tt), 0)
    col = lax.broadcasted_iota(jnp.int32, (tt, tt), 1)
    tri_sub = jnp.where((row >= col) & (row // chunk == col // chunk), 1.0, 0.0).astype(BF16)
    gh = gh_ref[0]
    ksl = [slice(g * dkh, (g + 1) * dkh) for g in range(group)]
    vsl = [slice(g * dvh, (g + 1) * dvh) for g in range(group)]
    rows = [slice(i * chunk, (i + 1) * chunk) for i in range(n)]

    def per_chunk(vecs):
        return jnp.concatenate([jnp.broadcast_to(v_, (chunk, v_.shape[1])) for v_ in vecs], axis=0)

    parts = _split3(la_ref[0])
    b = _dot(tri_sub, parts[0]) + _dot(tri_sub, parts[1]) + _dot(tri_sub, parts[2])
    total = [b[(i + 1) * chunk - 1:(i + 1) * chunk, :] for i in range(n)]
    start = [jnp.zeros_like(total[0])]
    for i in range(n):
        start.append(start[i] + total[i])

    qd = (q_ref[0] * q_scale) * jnp.exp(b)
    kf = k_ref[0]
    kd = kf * jnp.exp(per_chunk(total) - b)
    vb = v_ref[0].astype(BF16)
    q_dec = qd.astype(BF16)
    k_inv = (kf * jnp.exp(-b)).astype(BF16)
    q_state = (qd * per_chunk([jnp.exp(c_) for c_ in start[:n]])).astype(BF16)
    k_end = (kd * per_chunk([jnp.exp(start[n] - start[j + 1]) for j in range(n)])).astype(BF16)
    decay = jnp.exp(start[n])

    s = [s_scr[g] for g in range(group)]
    o_state = [_dot(q_state[:, ks], s_.astype(BF16)) for ks, s_ in zip(ksl, s)]
    upd = [_dot_tn(k_end[:, ks], vb[:, vs]) for ks, vs in zip(ksl, vsl)]
    for g in range(group):
        decay_col = jnp.transpose(jnp.broadcast_to(decay[:, ksl[g]], (LANES_V7X, dkh)))
        s_scr[g] = s[g] * jnp.tile(decay_col, (1, dvh // LANES_V7X)) + upd[g]

    keys = []
    for i in range(n):
        past = [(kd[rows[j], :] * jnp.exp(start[i] - start[j + 1])).astype(BF16) for j in range(i)]
        keys.append(jnp.concatenate(past + [k_inv[rows[i], :]], axis=0))
    scores = {}
    for i in range(n):
        width = (i + 1) * chunk
        visible = (lax.broadcasted_iota(jnp.int32, (chunk, width), 1)
                   <= lax.broadcasted_iota(jnp.int32, (chunk, width), 0) + i * chunk)
        for g in range(group):
            sc = _dot_nt(q_dec[rows[i], ksl[g]], keys[i][:, ksl[g]])
            scores[i, g] = jnp.where(visible, sc, 0.0).astype(BF16)
    intra = {(i, g): _dot(scores[i, g], vb[0:(i + 1) * chunk, vsl[g]]) for i in range(n) for g in range(group)}
    for g in range(group):
        o = o_state[g] + jnp.concatenate([intra[i, g] for i in range(n)], axis=0)
        ms = jnp.mean(o * o, axis=-1, keepdims=True)
        on = (o * lax.rsqrt(ms + RMS_EPS)) * gh
        rr = r_ref[0, :, vsl[g]]
        y_ref[0, :, vsl[g]] = (on * (rr * _sigmoid(rr))).astype(y_ref.dtype)

    @pl.when(t == pl.num_programs(2) - 1)
    def _():
        for g in range(group):
            sout_ref[0, 0, g] = s_scr[g]


def _gla_recurrence(proj, log_a, s0, s0_layer, g_head, layer, states, *, heads, dk, dv):
    bsz, seq, _ = proj.shape
    dkh, dvh = dk // heads, dv // heads
    group = GLA_HEAD_GROUP
    tt = _tile(seq, 256)
    assert tt % GLA_CHUNK == 0 and heads % group == 0 and (2 * dk) % (group * dvh) == 0
    hg = heads // group
    v_off = 2 * dk // (group * dvh)
    body = functools.partial(_gla_body, chunk=GLA_CHUNK, q_scale=dkh ** -0.5)
    return pl.pallas_call(
        body,
        grid=(bsz, hg, seq // tt),
        in_specs=[
            pl.BlockSpec((1, tt, group * dkh), lambda b, h, t: (b, t, h)),
            pl.BlockSpec((1, tt, group * dkh), lambda b, h, t: (b, t, hg + h)),
            pl.BlockSpec((1, tt, group * dvh), lambda b, h, t: (b, t, v_off + h)),
            pl.BlockSpec((1, tt, group * dvh), lambda b, h, t: (b, t, v_off + hg + h)),
            pl.BlockSpec((1, tt, group * dkh), lambda b, h, t: (b, t, h)),
            pl.BlockSpec((1, 1, group, dkh, dvh), lambda b, h, t: (s0_layer, b, h, 0, 0)),
            pl.BlockSpec((1, 1, dvh), lambda b, h, t: (layer, 0, 0)),
            pl.BlockSpec(memory_space=pl.ANY),
        ],
        out_specs=[
            pl.BlockSpec((1, tt, group * dvh), lambda b, h, t: (b, t, h)),
            pl.BlockSpec((1, 1, group, dkh, dvh), lambda b, h, t: (layer, b, h, 0, 0)),
        ],
        out_shape=[
            jax.ShapeDtypeStruct((bsz, seq, dv), BF16),
            jax.ShapeDtypeStruct(states.shape, F32),
        ],
        scratch_shapes=[pltpu.VMEM((group, dkh, dvh), F32)],
        input_output_aliases={7: 1},
        compiler_params=_params("parallel", "parallel", "arbitrary"),
        name="gla_recurrence",
    )(proj, proj, proj, proj, log_a, s0, g_head, states)


def _gla_mixer(xp, xs, hp, hs, state, new_p, new_s, layer, seq_p, seq_s, w_in, wg1_bf, wgate_bf, b_gate, g_head, w_out,
               *, heads, dk, dv):
    proj_p, proj_s = _matmul(xp, jnp.swapaxes(w_in, 1, 2), layer, n_out=2 * dk + 2 * dv, side=xs,
                             w_transposed=True, name="gla_in")

    def recur(xn, proj, seq, s0, s0_layer, new):
        bsz = xn.shape[0] // seq
        log_a = _gla_gate(xn, wg1_bf, wgate_bf, b_gate, layer).reshape(bsz, seq, dk)
        proj = proj.reshape(bsz, seq, -1)
        pad = (-seq) % GLA_CHUNK
        if pad:
            proj = jnp.pad(proj, ((0, 0), (0, pad), (0, 0)))
            log_a = jnp.pad(log_a, ((0, 0), (0, pad), (0, 0)))
        y, new = _gla_recurrence(proj, log_a, s0, s0_layer, g_head, layer, new, heads=heads, dk=dk, dv=dv)
        return y[:, :seq].reshape(xn.shape[0], dv), new

    zero_state = jnp.zeros((1, xp.shape[0] // seq_p) + state.shape[2:], F32)
    yp, sp = recur(xp, proj_p, seq_p, zero_state, 0, new_p)
    ys, ss = recur(xs, proj_s, seq_s, state, layer, new_s)
    hp, hs = _matmul(yp, w_out, layer, n_out=w_out.shape[2], res=hp, side=ys, side_res=hs, name="gla_out")
    return hp, hs, sp, ss


def _sb_prompt_body(bias_ref, q_ref, k_ref, v_ref, o_ref, kb_scr, vb_scr, *, scale):
    h = pl.program_id(1)
    i = pl.program_id(2)
    blk, rows = SB_BLOCK, SB_ROWS
    n_sub = blk // rows
    hd = q_ref.shape[2]

    @pl.when(i == 0)
    def _():
        kb_scr[...] = k_ref[0].astype(BF16)
        vb_scr[...] = v_ref[0].astype(BF16)

    bias = bias_ref[h]
    r_io = lax.broadcasted_iota(jnp.int32, (blk, blk), 0)
    c_io = lax.broadcasted_iota(jnp.int32, (blk, blk), 1)
    upper = jnp.where(r_io > c_io, -1.0, 0.0).astype(BF16)
    upper2 = jnp.concatenate([upper, upper], axis=0)
    sub_r = lax.broadcasted_iota(jnp.int32, (rows, blk), 0)
    sub_c = lax.broadcasted_iota(jnp.int32, (rows, blk), 1)

    def sweep(jobs, state):
        state = dict(state)
        keys = [pl.ds(pl.multiple_of(kb * blk, blk), blk) for _, kb, _ in jobs]
        z = [_dot_nt(q_ref[0, r0:r0 + rows, :], kb_scr[ks, :]) * scale + bias for (r0, _, _), ks in zip(jobs, keys)]
        sp = [_softplus(zz) for zz in z]
        vis = [sub_c < sub_r + (r0 % blk) if m else None for r0, _, m in jobs]
        nlk = [s_ if v_ is None else jnp.where(v_, s_, 0.0) for s_, v_ in zip(sp, vis)]
        later = [_dot(jnp.concatenate(_split2(l_), axis=1), upper2) for l_ in nlk]
        for (r0, _, _), ks, zz, s_, v_, l_, lt in zip(jobs, keys, z, sp, vis, nlk, later):
            tail, o = state[r0]
            a = jnp.exp((jnp.concatenate([tail] * (blk // LANES_V7X), axis=1) + lt) + (zz - s_))
            if v_ is not None:
                a = jnp.where(v_, a, 0.0)
            o = o + _dot(a.astype(BF16), vb_scr[ks, :])
            tail = tail + jnp.broadcast_to(lt[:, 0:1] - l_[:, 0:1], tail.shape)
            state[r0] = (tail, o)
        return state

    zero = (jnp.zeros((rows, LANES_V7X), F32), jnp.zeros((rows, hd), F32))
    rows_a = [s * rows for s in range(n_sub)]
    rows_b = [blk + s * rows for s in range(n_sub)]
    all_rows = rows_a + rows_b
    state = sweep([(r, 2 * i + 1, True) for r in rows_b] + [(r, 2 * i, True) for r in rows_a]
                  + [(r, 2 * i, False) for r in rows_b], {r: zero for r in all_rows})

    def step(jj, carry):
        kb = 2 * i - 1 - jj
        state = sweep([(r, kb, False) for r in all_rows], dict(zip(all_rows, carry)))
        return tuple(state[r] for r in all_rows)

    final = lax.fori_loop(0, 2 * i, step, tuple(state[r] for r in all_rows))
    for r, (_, o) in zip(all_rows, final):
        o_ref[0, r:r + rows, :] = o.astype(o_ref.dtype)


def _sb_prompt(q, k, v, bias, *, heads):
    bsz, seq, d = q.shape
    hd = d // heads
    tq = 2 * SB_BLOCK
    assert hd == LANES_V7X and seq % tq == 0
    body = functools.partial(_sb_prompt_body, scale=hd ** -0.5)
    return pl.pallas_call(
        body,
        grid=(bsz, heads, seq // tq),
        in_specs=[
            pl.BlockSpec(memory_space=pltpu.SMEM),
            pl.BlockSpec((1, tq, hd), lambda b, h, i: (b, i, h)),
            pl.BlockSpec((1, seq, hd), lambda b, h, i: (b, 0, h)),
            pl.BlockSpec((1, seq, hd), lambda b, h, i: (b, 0, h)),
        ],
        out_specs=pl.BlockSpec((1, tq, hd), lambda b, h, i: (b, i, h)),
        out_shape=jax.ShapeDtypeStruct((bsz, seq, d), BF16),
        scratch_shapes=[pltpu.VMEM((seq, hd), BF16), pltpu.VMEM((seq, hd), BF16)],
        compiler_params=_params("parallel", "parallel", "arbitrary"),
        name="sb_prompt",
    )(bias, q, k, v)


def _sb_sample_body(pt_ref, qbd_ref, bias_ref, kn_ref, vn_ref, *refs, heads, n_new, scale, pages_per_step):
    del pt_ref
    kp_refs, vp_refs = refs[:pages_per_step], refs[pages_per_step:2 * pages_per_step]
    o_ref, acc_scr, tail_scr = refs[2 * pages_per_step:]
    p = pl.program_id(1)
    page, nl = kn_ref.shape[1], qbd_ref.shape[2]
    lower = (lax.broadcasted_iota(jnp.int32, (page, page), 1) > lax.broadcasted_iota(jnp.int32, (page, page), 0)).astype(BF16)

    def blocks(kbs, vbs, vis):
        z = [_dot(kb, qbd_ref[0]) * scale + bias_ref[...] for kb in kbs]
        sp = [_softplus(zz) for zz in z]
        lk = [-s_ if vis is None else jnp.where(vis, -s_, 0.0) for s_ in sp]
        later = []
        for l_ in lk:
            lk_hi, lk_mid, lk_lo = _split3(l_)
            later.append(_dot(lower, lk_hi) + _dot(lower, lk_mid) + _dot(lower, lk_lo))
        tail = tail_scr[0:1, :]
        acc = acc_scr[...]
        for zz, s_, l_, lt, vb in zip(z, sp, lk, later, vbs):
            a = jnp.exp((tail + lt) + (zz - s_))
            if vis is not None:
                a = jnp.where(vis, a, 0.0)
            acc = acc + _dot(jnp.transpose(a).astype(BF16), vb)
            tail = tail + jnp.sum(l_, axis=0, keepdims=True)
        acc_scr[...] = acc
        tail_scr[0:1, :] = tail

    def heads_to_lanes(ref):
        by_head = jnp.swapaxes(ref[0, 0], 0, 1)
        return jnp.concatenate([by_head[h].astype(BF16) for h in range(heads)], axis=1)

    @pl.when(p == 0)
    def _():
        acc_scr[...] = jnp.zeros_like(acc_scr)
        tail_scr[...] = jnp.zeros_like(tail_scr)
        key = lax.broadcasted_iota(jnp.int32, (page, nl), 0)
        tok = lax.broadcasted_iota(jnp.int32, (page, nl), 1) // heads
        blocks([kn_ref[0].astype(BF16)], [vn_ref[0].astype(BF16)], (key < tok) & (key < n_new))

    @pl.when(p > 0)
    def _():
        blocks([heads_to_lanes(r) for r in kp_refs], [heads_to_lanes(r) for r in vp_refs], None)

    @pl.when(p == pl.num_programs(1) - 1)
    def _():
        d = acc_scr.shape[1]
        hd = d // heads
        own = lax.broadcasted_iota(jnp.int32, (heads, d), 0) == lax.broadcasted_iota(jnp.int32, (heads, d), 1) // hd
        for t in range(n_new):
            rows = acc_scr[t * heads:(t + 1) * heads, :]
            o_ref[0, t:t + 1, :] = jnp.sum(jnp.where(own, rows, 0.0), axis=0, keepdims=True).astype(o_ref.dtype)


def _sb_sample(q, k_new, v_new, cache_k, cache_v, layer, page_table, bias, *, heads):
    bd, t_new, d = q.shape
    hd = d // heads
    page = cache_k.shape[2]
    n_pages = page_table.shape[1]
    nl = LANES_V7X
    assert heads * t_new <= nl and heads % 8 == 0 and hd % LANES_V7X == 0 and t_new <= page
    q4 = q.reshape(bd, t_new, heads, hd)
    qbd = jnp.einsum("bthx,hg->bhxtg", q4, jnp.eye(heads, dtype=q.dtype)).reshape(bd, d, t_new * heads)
    qbd = jnp.pad(qbd, ((0, 0), (0, 0), (0, nl - t_new * heads))).astype(BF16)
    bias_l = jnp.pad(jnp.tile(bias, t_new), (0, nl - t_new * heads)).reshape(1, nl)
    kn = jnp.pad(k_new, ((0, 0), (0, page - t_new), (0, 0)))
    vn = jnp.pad(v_new, ((0, 0), (0, page - t_new), (0, 0)))

    pps = SB_PAGES_PER_STEP
    assert n_pages % pps == 0

    def page_map(slot):
        return lambda b, p, pt: (layer, pt[b, n_pages - pps * jnp.maximum(p, 1) + (pps - 1 - slot)], 0, 0, 0)

    page_specs = [pl.BlockSpec((1, 1, page, heads, hd), page_map(slot)) for slot in range(pps)]
    body = functools.partial(_sb_sample_body, heads=heads, n_new=t_new, scale=hd ** -0.5, pages_per_step=pps)
    grid_spec = pltpu.PrefetchScalarGridSpec(
        num_scalar_prefetch=1,
        grid=(bd, n_pages // pps + 1),
        in_specs=[
            pl.BlockSpec((1, d, nl), lambda b, p, pt: (b, 0, 0)),
            pl.BlockSpec((1, nl), lambda b, p, pt: (0, 0)),
            pl.BlockSpec((1, page, d), lambda b, p, pt: (b, 0, 0)),
            pl.BlockSpec((1, page, d), lambda b, p, pt: (b, 0, 0)),
        ] + page_specs + page_specs,
        out_specs=pl.BlockSpec((1, t_new, d), lambda b, p, pt: (b, 0, 0)),
        scratch_shapes=[pltpu.VMEM((nl, d), F32), pltpu.VMEM((8, nl), F32)],
    )
    return pl.pallas_call(
        body,
        grid_spec=grid_spec,
        out_shape=jax.ShapeDtypeStruct((bd, t_new, d), F32),
        compiler_params=_params("parallel", "arbitrary"),
        name="sb_sample",
    )(page_table, qbd, bias_l, kn, vn, *([cache_k] * pps), *([cache_v] * pps))


def _conv_prompt_body(x_ref, xs_ref, wb_ref, wc_ref, wh_ref, wconv_ref, buf0_ref, y_ref, st_ref, sb_ref, sc_ref, sh_ref,
                      u_scr, *, blocks_per_seq):
    m = pl.program_id(1)
    tm = x_ref.shape[0]
    x = x_ref[...]
    wb, wc, wh = wb_ref[0].astype(BF16), wc_ref[0].astype(BF16), wh_ref[0].astype(BF16)
    bg = _dot(x, wb)
    u = _dot(x, wc) * _dot(x, wh)

    @pl.when(m % blocks_per_seq == 0)
    def _():
        u_scr[6:8, :] = buf0_ref[0]

    @pl.when(m == 0)
    def _():
        xs = xs_ref[...]
        sb_ref[...] = _dot(xs, wb)
        sc_ref[...] = _dot(xs, wc)
        sh_ref[...] = _dot(xs, wh)

    u_scr[8:8 + tm, :] = u
    wconv = wconv_ref[0]
    conv = wconv[0:1, :] * u_scr[6:6 + tm, :] + wconv[1:2, :] * u_scr[7:7 + tm, :] + wconv[2:3, :] * u
    y_ref[...] = (bg * conv).astype(y_ref.dtype)
    st_ref[0] = u_scr[tm + 6:tm + 8, :]
    u_scr[0:8, :] = u_scr[tm:tm + 8, :]


def _conv_prompt(xn, side, w_in, w_conv, layer, buf0, seq):
    m, d = xn.shape
    ms = side.shape[0]
    tm = _tile(seq, 1024)
    tn = _tile(d, 256)
    nb = d // tn
    blocks_per_seq = seq // tm
    body = functools.partial(_conv_prompt_body, blocks_per_seq=blocks_per_seq)
    side_spec = pl.BlockSpec((ms, tn), lambda n, i: (0, n))
    side_shape = jax.ShapeDtypeStruct((ms, d), F32)
    return pl.pallas_call(
        body,
        grid=(nb, m // tm),
        in_specs=[
            pl.BlockSpec((tm, d), lambda n, i: (i, 0)),
            pl.BlockSpec((ms, d), lambda n, i: (0, 0)),
            pl.BlockSpec((1, d, tn), lambda n, i: (layer, 0, n)),
            pl.BlockSpec((1, d, tn), lambda n, i: (layer, 0, nb + n)),
            pl.BlockSpec((1, d, tn), lambda n, i: (layer, 0, 2 * nb + n)),
            pl.BlockSpec((1, w_conv.shape[1], tn), lambda n, i: (layer, 0, n)),
            pl.BlockSpec((1, 2, tn), lambda n, i: (i // blocks_per_seq, 0, n)),
        ],
        out_specs=[
            pl.BlockSpec((tm, tn), lambda n, i: (i, n)),
            pl.BlockSpec((1, 2, tn), lambda n, i: (i // blocks_per_seq, 0, n)),
            side_spec, side_spec, side_spec,
        ],
        out_shape=[
            jax.ShapeDtypeStruct((m, d), BF16),
            jax.ShapeDtypeStruct((m // seq, 2, d), F32),
            side_shape, side_shape, side_shape,
        ],
        scratch_shapes=[pltpu.VMEM((tm + 8, tn), F32)],
        compiler_params=_params("arbitrary", "arbitrary"),
        name="conv_prompt",
    )(xn, side, w_in, w_in, w_in, w_conv, buf0)


def _conv_sample_body(bg_ref, cg_ref, h_ref, buf0_ref, wconv_ref, y_ref, st_ref, *, seq):
    taps = [buf0_ref[0], buf0_ref[1]]
    for t in range(seq):
        taps.append(cg_ref[t] * h_ref[t])
    for t in range(seq):
        conv = wconv_ref[0:1, :] * taps[t] + wconv_ref[1:2, :] * taps[t + 1] + wconv_ref[2:3, :] * taps[t + 2]
        y_ref[t] = (bg_ref[t] * conv).astype(y_ref.dtype)
    st_ref[0] = taps[seq]
    st_ref[1] = taps[seq + 1]


def _conv_sample(bg_tm, cg_tm, h_tm, buf0_tm, w_conv):
    seq, bd, d = bg_tm.shape
    return pl.pallas_call(
        functools.partial(_conv_sample_body, seq=seq),
        out_shape=[jax.ShapeDtypeStruct((seq, bd, d), BF16), jax.ShapeDtypeStruct((2, bd, d), F32)],
        compiler_params=pltpu.CompilerParams(vmem_limit_bytes=VMEM_LIMIT_V7X),
        name="conv_sample",
    )(bg_tm, cg_tm, h_tm, buf0_tm, w_conv)


def kernel(x_prompt, x_sample, state_gla, cache_sb_k, cache_sb_v, state_conv, cache_mem_k, cache_mem_v, page_table, mem_prompt, g_mix, g_xattn, g_mem, g_ffn, g_final, w_gla_in, w_gla_gate, b_gla_gate, g_gla_head, w_gla_out, w_sb_in, b_sb, w_sb_out, w_conv_in, w_conv, w_conv_out, w_xq, w_xkv, w_xo, w_ffn_in, w_ffn_out):
    bsz, seq, d = x_prompt.shape
    bd, t_new, _ = x_sample.shape
    depth = g_mix.shape[0]
    n_mem = mem_prompt.shape[1]
    gla_heads = state_gla.shape[2]
    dk = w_gla_gate.shape[2]
    dv = w_gla_out.shape[1]
    sb_heads = b_sb.shape[1]
    x_heads = cache_mem_k.shape[3]
    xw = w_xq.shape[2]
    mp, ms = bsz * seq, bd * t_new

    bf = lambda w: w.astype(BF16)
    rank = w_gla_gate.shape[1]
    w_gla_g1 = bf(jnp.pad(w_gla_in[:, :, 2 * dk + 2 * dv:], ((0, 0), (0, 0), (0, GLA_GATE_PAD - rank))))
    w_gla_gate_p = bf(jnp.pad(w_gla_gate, ((0, 0), (0, GLA_GATE_PAD - rank), (0, 0))))
    w_xq_bf, w_xkv_bf, w_xo_bf = bf(w_xq), bf(w_xkv), bf(w_xo)
    b_gla_gate3 = b_gla_gate.reshape(-1, 1, dk)
    g_gla_head3 = g_gla_head.reshape(-1, 1, dv // gla_heads)
    g_xattn3 = g_xattn.reshape(depth, 1, d)
    g_ffn3 = g_ffn.reshape(depth, 1, d)
    cache_mem_k3 = cache_mem_k.reshape(depth, bd * n_mem, xw)
    cache_mem_v3 = cache_mem_v.reshape(depth, bd * n_mem, xw)

    hp = x_prompt.reshape(mp, d)
    hs = x_sample.reshape(ms, d)
    mem_k, mem_v = _mem_kv(mem_prompt.reshape(bsz * n_mem, d), g_mem, w_xkv_bf)

    gla_p = jnp.zeros((state_gla.shape[0], bsz) + state_gla.shape[2:], F32)
    gla_s = jnp.zeros(state_gla.shape, F32)
    sbk_p, sbv_p, sbk_s, sbv_s, conv_p, conv_s = [], [], [], [], [], []
    for i in range(depth):
        kind, j = i % 3, i // 3
        xp = _rmsnorm(hp, g_mix[i], BF16)
        xs = _rmsnorm(hs, g_mix[i], BF16)
        if kind == 0:
            hp, hs, gla_p, gla_s = _gla_mixer(xp, xs, hp, hs, state_gla, gla_p, gla_s, j, seq, t_new, w_gla_in, w_gla_g1,
                                              w_gla_gate_p, b_gla_gate3, g_gla_head3, w_gla_out,
                                              heads=gla_heads, dk=dk, dv=dv)
        elif kind == 1:
            qp, qs = _matmul(xp, w_sb_in, j, n_out=d, col_off=0, out_dtype=BF16, side=xs, name="sb_q")
            kp, kn = _matmul(xp, w_sb_in, j, n_out=d, col_off=d, side=xs, name="sb_k")
            vp, vn = _matmul(xp, w_sb_in, j, n_out=d, col_off=2 * d, side=xs, name="sb_v")
            qs, kn, vn = (a.reshape(bd, t_new, d) for a in (qs, kn, vn))
            op = _sb_prompt(qp.reshape(bsz, seq, d), kp.reshape(bsz, seq, d), vp.reshape(bsz, seq, d), b_sb[j],
                            heads=sb_heads)
            os_ = _sb_sample(qs, kn, vn, cache_sb_k, cache_sb_v, j, page_table, b_sb[j], heads=sb_heads)
            hp, hs = _matmul(op.reshape(mp, d), w_sb_out, j, n_out=d, res=hp,
                             side=os_.reshape(ms, d).astype(BF16), side_res=hs, name="sb_out")
            sbk_p.append(kp)
            sbv_p.append(vp)
            sbk_s.append(kn)
            sbv_s.append(vn)
        else:
            yp, bp, *side_proj = _conv_prompt(xp, xs, w_conv_in, w_conv, j, jnp.zeros((bsz, 2, d), F32), seq)
            ys, bs = _conv_sample(*(a.reshape(bd, t_new, d).transpose(1, 0, 2) for a in side_proj),
                                  state_conv[j].transpose(1, 0, 2), w_conv[j])
            hp, hs = _matmul(yp, w_conv_out, j, n_out=d, res=hp,
                             side=ys.transpose(1, 0, 2).reshape(ms, d), side_res=hs, name="conv_out")
            conv_p.append(bp)
            conv_s.append(bs.transpose(1, 0, 2))
        hp, xp_ffn = _xattn(hp, g_xattn3, g_ffn3, w_xq_bf, mem_k, mem_v, w_xo_bf, i,
                            heads=x_heads, rows_per_seq=seq, n_mem=n_mem, grouped=False)
        hs, xs_ffn = _xattn(hs, g_xattn3, g_ffn3, w_xq_bf, cache_mem_k3, cache_mem_v3, w_xo_bf, i,
                            heads=x_heads, rows_per_seq=t_new, n_mem=n_mem, grouped=True)
        ap, as_, w_out_bf = _swiglu_in(xp_ffn, xs_ffn, w_ffn_in, w_ffn_out, i, tm_pref=2048, tf_pref=256)
        hp, hs = _matmul(ap, w_out_bf[None], 0, n_out=d, res=hp, side=as_, side_res=hs,
                         tm_pref=512, tn_pref=512, cols_outer=True, name="ffn_out")

    y_prompt = _rmsnorm(hp, g_final, F32).reshape(bsz, seq, d)
    y_sample = _rmsnorm(hs, g_final, F32).reshape(bd, t_new, d)
    hd = d // sb_heads
    sb_p = lambda xs_: jnp.stack(xs_).reshape(-1, bsz, seq, sb_heads, hd)
    sb_s = lambda xs_: jnp.stack(xs_).reshape(-1, bd, t_new, sb_heads, hd)
    xhd = xw // x_heads
    return (y_prompt, y_sample,
            gla_p, gla_s,
            sb_p(sbk_p), sb_p(sbv_p), sb_s(sbk_s), sb_s(sbv_s),
            jnp.stack(conv_p), jnp.stack(conv_s),
            mem_k.reshape(depth, bsz, n_mem, x_heads, xhd), mem_v.reshape(depth, bsz, n_mem, x_heads, xhd))
```

```python
import functools

import jax
import jax.numpy as jnp
from jax import lax
from jax.experimental import pallas as pl
from jax.experimental.pallas import tpu as pltpu

F32 = jnp.float32
BF16 = jnp.bfloat16

LANES_V7X = 128
VMEM_LIMIT_V7X = 56 * 1024 * 1024
RMS_EPS = 1e-6
GLA_TAU = 16.0
GLA_CHUNK = 32
GLA_GATE_PAD = LANES_V7X
GLA_HEAD_GROUP = 2
SB_BLOCK = 256
SB_ROWS = 128
SB_PAGES_PER_STEP = 4
NEG_BIG = -1e30


def _params(*sem):
    return pltpu.CompilerParams(dimension_semantics=sem, vmem_limit_bytes=VMEM_LIMIT_V7X)


def _tile(n, pref):
    if n <= pref:
        return n
    t = (pref // LANES_V7X) * LANES_V7X
    while t > LANES_V7X and n % t:
        t -= LANES_V7X
    assert n % t == 0, (n, pref)
    return t


def _softplus(z):
    return jnp.maximum(z, 0.0) + jnp.log(1.0 + jnp.exp(-jnp.abs(z)))


def _sigmoid(z):
    return 1.0 / (1.0 + jnp.exp(-z))


def _split2(x):
    hi = x.astype(BF16)
    lo = (x - hi.astype(F32)).astype(BF16)
    return hi, lo


def _split3(x):
    hi = x.astype(BF16)
    r1 = x - hi.astype(F32)
    mid = r1.astype(BF16)
    lo = (r1 - mid.astype(F32)).astype(BF16)
    return hi, mid, lo


def _dot(a, b):
    return jnp.dot(a, b, preferred_element_type=F32)


def _dot_nt(a, b):
    return lax.dot_general(a, b, (((1,), (1,)), ((), ())), preferred_element_type=F32)


def _dot_tn(a, b):
    return lax.dot_general(a, b, (((0,), (0,)), ((), ())), preferred_element_type=F32)


def _rmsnorm_body(x_ref, g_ref, o_ref):
    x = x_ref[...]
    ms = jnp.mean(x * x, axis=-1, keepdims=True)
    o_ref[...] = ((x * lax.rsqrt(ms + RMS_EPS)) * g_ref[...]).astype(o_ref.dtype)


def _rmsnorm(x, g, out_dtype):
    m, d = x.shape
    tm = _tile(m, 512)
    return pl.pallas_call(
        _rmsnorm_body,
        grid=(m // tm,),
        in_specs=[pl.BlockSpec((tm, d), lambda i: (i, 0)), pl.BlockSpec((1, d), lambda i: (0, 0))],
        out_specs=pl.BlockSpec((tm, d), lambda i: (i, 0)),
        out_shape=jax.ShapeDtypeStruct((m, d), out_dtype),
        compiler_params=_params("parallel"),
        name="rmsnorm",
    )(x, g.reshape(1, d))


def _mm_body(*refs, has_res, has_side, has_side_res, w_transposed, row_axis):
    dot = _dot_nt if w_transposed else _dot
    it = iter(refs)
    x_ref, w_ref = next(it), next(it)
    r_ref = next(it) if has_res else None
    xs_ref = next(it) if has_side else None
    rs_ref = next(it) if has_side_res else None
    o_ref = next(it)
    os_ref = next(it) if has_side else None
    wb = w_ref[0].astype(BF16)
    acc = dot(x_ref[...], wb)
    if has_res:
        acc = acc + r_ref[...]
    o_ref[...] = acc.astype(o_ref.dtype)
    if has_side:
        @pl.when(pl.program_id(row_axis) == 0)
        def _():
            side = dot(xs_ref[...], wb)
            if has_side_res:
                side = side + rs_ref[...]
            os_ref[...] = side.astype(os_ref.dtype)


def _side_col(i, j, last):
    return jnp.where(i == 0, j, last)


def _matmul(x, w, layer, *, n_out, col_off=0, res=None, out_dtype=F32, side=None, side_res=None, side_dtype=F32,
            tm_pref=2048, tn_pref=256, single_buffer_x=True, w_transposed=False, cols_outer=False, name="matmul"):
    m, k = x.shape
    tm, tn = _tile(m, tm_pref), _tile(n_out, tn_pref)
    assert col_off % tn == 0
    off = col_off // tn
    nj = n_out // tn
    if cols_outer:
        grid = (nj, m // tm)
        ij = lambda a, b: (b, a)
        side_col = lambda a, b: a
        w_mode = dict(pipeline_mode=pl.Buffered(1))
    else:
        grid = (m // tm, nj)
        ij = lambda a, b: (a, b)
        side_col = lambda a, b: _side_col(a, b, nj - 1)
        w_mode = {}
    x_mode = dict(pipeline_mode=pl.Buffered(1)) if single_buffer_x else {}
    w_spec = (pl.BlockSpec((1, tn, k), lambda a, b: (layer, ij(a, b)[1] + off, 0), **w_mode) if w_transposed else
              pl.BlockSpec((1, k, tn), lambda a, b: (layer, 0, ij(a, b)[1] + off), **w_mode))
    in_specs = [pl.BlockSpec((tm, k), lambda a, b: (ij(a, b)[0], 0), **x_mode), w_spec]
    args = [x, w]
    out_specs = [pl.BlockSpec((tm, tn), lambda a, b: ij(a, b))]
    out_shape = [jax.ShapeDtypeStruct((m, n_out), out_dtype)]
    if res is not None:
        in_specs.append(pl.BlockSpec((tm, tn), lambda a, b: ij(a, b)))
        args.append(res)
    if side is not None:
        ms = side.shape[0]
        in_specs.append(pl.BlockSpec((ms, k), lambda a, b: (0, 0)))
        args.append(side)
        if side_res is not None:
            in_specs.append(pl.BlockSpec((ms, tn), lambda a, b: (0, side_col(a, b))))
            args.append(side_res)
        out_specs.append(pl.BlockSpec((ms, tn), lambda a, b: (0, side_col(a, b))))
        out_shape.append(jax.ShapeDtypeStruct((ms, n_out), side_dtype))
    body = functools.partial(_mm_body, has_res=res is not None, has_side=side is not None,
                             has_side_res=side_res is not None, w_transposed=w_transposed,
                             row_axis=1 if cols_outer else 0)
    outs = pl.pallas_call(
        body,
        grid=grid,
        in_specs=in_specs,
        out_specs=out_specs,
        out_shape=out_shape,
        compiler_params=_params("arbitrary", "arbitrary"),
        name=name,
    )(*args)
    return outs if side is not None else outs[0]


def _mem_kv_body(x_ref, g_ref, w_ref, k_ref, v_ref):
    x = x_ref[...]
    ms = jnp.mean(x * x, axis=-1, keepdims=True)
    xn = ((x * lax.rsqrt(ms + RMS_EPS)) * g_ref[0]).astype(BF16)
    kv = _dot(xn, w_ref[0])
    half = kv.shape[1] // 2
    k_ref[0] = kv[:, :half]
    v_ref[0] = kv[:, half:]


def _mem_kv(mem, g_mem, w_xkv_bf):
    rows, d = mem.shape
    depth, _, two_w = w_xkv_bf.shape
    xw = two_w // 2
    tm = _tile(rows, 512)
    out = jax.ShapeDtypeStruct((depth, rows, xw), F32)
    return pl.pallas_call(
        _mem_kv_body,
        grid=(depth, rows // tm),
        in_specs=[
            pl.BlockSpec((tm, d), lambda i, m: (m, 0)),
            pl.BlockSpec((1, 1, d), lambda i, m: (i, 0, 0)),
            pl.BlockSpec((1, d, two_w), lambda i, m: (i, 0, 0)),
        ],
        out_specs=[pl.BlockSpec((1, tm, xw), lambda i, m: (i, m, 0))] * 2,
        out_shape=[out, out],
        compiler_params=_params("parallel", "parallel"),
        name="mem_kv",
    )(mem, g_mem.reshape(depth, 1, d), w_xkv_bf)


def _xattn_body(x_ref, g_ref, g_next_ref, wq_ref, mk_ref, mv_ref, wo_ref, o_ref, on_ref, *, heads, scale,
                rows_per_group, mem_per_group):
    x = x_ref[...]
    ms = jnp.mean(x * x, axis=-1, keepdims=True)
    xn = ((x * lax.rsqrt(ms + RMS_EPS)) * g_ref[0]).astype(BF16)
    q = _dot(xn, wq_ref[0])
    mk = mk_ref[0].astype(BF16)
    mv = mv_ref[0].astype(BF16)
    hd = q.shape[1] // heads
    outs = []
    for h in range(heads):
        sl = slice(h * hd, (h + 1) * hd)
        s = _dot_nt(q[:, sl].astype(BF16), mk[:, sl]) * scale
        if rows_per_group is not None:
            rg = lax.broadcasted_iota(jnp.int32, s.shape, 0) // rows_per_group
            cg = lax.broadcasted_iota(jnp.int32, s.shape, 1) // mem_per_group
            s = jnp.where(rg == cg, s, NEG_BIG)
        e = jnp.exp(s - jnp.max(s, axis=-1, keepdims=True))
        p = e / jnp.sum(e, axis=-1, keepdims=True)
        outs.append(_dot(p.astype(BF16), mv[:, sl]))
    o = jnp.concatenate(outs, axis=-1).astype(BF16)
    y = _dot(o, wo_ref[0]) + x
    o_ref[...] = y
    ms_y = jnp.mean(y * y, axis=-1, keepdims=True)
    on_ref[...] = ((y * lax.rsqrt(ms_y + RMS_EPS)) * g_next_ref[0]).astype(on_ref.dtype)


def _xattn(h, g, g_next, wq_bf, mk, mv, wo_bf, layer, *, heads, rows_per_seq, n_mem, grouped):
    m, d = h.shape
    xw = wq_bf.shape[2]
    scale = (xw // heads) ** -0.5
    if grouped:
        tm, mem_rows = m, mk.shape[1]
        mem_map = lambda i: (layer, 0, 0)
        rpg = rows_per_seq
    else:
        tm, mem_rows = _tile(rows_per_seq, 256), n_mem
        blocks_per_seq = rows_per_seq // tm
        mem_map = lambda i: (layer, i // blocks_per_seq, 0)
        rpg = None
    body = functools.partial(_xattn_body, heads=heads, scale=scale, rows_per_group=rpg, mem_per_group=n_mem)
    return pl.pallas_call(
        body,
        grid=(m // tm,),
        in_specs=[
            pl.BlockSpec((tm, d), lambda i: (i, 0)),
            pl.BlockSpec((1, 1, d), lambda i: (layer, 0, 0)),
            pl.BlockSpec((1, 1, d), lambda i: (layer, 0, 0)),
            pl.BlockSpec((1, d, xw), lambda i: (layer, 0, 0)),
            pl.BlockSpec((1, mem_rows, xw), mem_map),
            pl.BlockSpec((1, mem_rows, xw), mem_map),
            pl.BlockSpec((1, xw, d), lambda i: (layer, 0, 0)),
        ],
        out_specs=[pl.BlockSpec((tm, d), lambda i: (i, 0))] * 2,
        out_shape=[jax.ShapeDtypeStruct((m, d), F32), jax.ShapeDtypeStruct((m, d), BF16)],
        compiler_params=_params("parallel"),
        name="xattn",
    )(h, g, g_next, wq_bf, mk, mv, wo_bf)


def _swiglu_in_body(x_ref, xs_ref, wg_ref, wu_ref, wnext_ref, o_ref, os_ref, wnext_bf_ref):
    wg = wg_ref[0].astype(BF16)
    wu = wu_ref[0].astype(BF16)

    def act(x):
        gate = _dot(x, wg)
        return (gate * _sigmoid(gate)) * _dot(x, wu)

    o_ref[...] = act(x_ref[...]).astype(o_ref.dtype)

    @pl.when(pl.program_id(0) == 0)
    def _():
        os_ref[...] = act(xs_ref[...]).astype(os_ref.dtype)
        wnext_bf_ref[...] = wnext_ref[0].astype(BF16)


def _swiglu_in(xn, side, w_in, w_next, layer, *, tm_pref, tf_pref):
    m, d = xn.shape
    ms = side.shape[0]
    f = w_in.shape[2] // 2
    tm, tf = _tile(m, tm_pref), _tile(f, tf_pref)
    nf = f // tf
    d_next = w_next.shape[2]
    return pl.pallas_call(
        _swiglu_in_body,
        grid=(m // tm, nf),
        in_specs=[
            pl.BlockSpec((tm, d), lambda i, j: (i, 0), pipeline_mode=pl.Buffered(1)),
            pl.BlockSpec((ms, d), lambda i, j: (0, 0)),
            pl.BlockSpec((1, d, tf), lambda i, j: (layer, 0, j)),
            pl.BlockSpec((1, d, tf), lambda i, j: (layer, 0, j + nf)),
            pl.BlockSpec((1, tf, d_next), lambda i, j: (layer, _side_col(i, j, nf - 1), 0)),
        ],
        out_specs=[pl.BlockSpec((tm, tf), lambda i, j: (i, j)),
                   pl.BlockSpec((ms, tf), lambda i, j: (0, _side_col(i, j, nf - 1))),
                   pl.BlockSpec((tf, d_next), lambda i, j: (_side_col(i, j, nf - 1), 0))],
        out_shape=[jax.ShapeDtypeStruct((m, f), BF16), jax.ShapeDtypeStruct((ms, f), BF16),
                   jax.ShapeDtypeStruct((f, d_next), BF16)],
        compiler_params=_params("arbitrary", "arbitrary"),
        name="swiglu_in",
    )(xn, side, w_in, w_in, w_next)


def _gla_gate_body(x_ref, wg1_ref, wgate_ref, b_ref, o_ref):
    g1 = _dot(x_ref[...], wg1_ref[0])
    z = _dot(g1.astype(BF16), wgate_ref[0]) + b_ref[0]
    log_sig = jnp.minimum(z, 0.0) - jnp.log1p(jnp.exp(-jnp.abs(z)))
    o_ref[...] = log_sig / GLA_TAU


def _gla_gate(xn, wg1_bf, wgate_bf, b_gate, layer):
    m, d = xn.shape
    dk = wgate_bf.shape[2]
    tm = _tile(m, 512)
    return pl.pallas_call(
        _gla_gate_body,
        grid=(m // tm,),
        in_specs=[
            pl.BlockSpec((tm, d), lambda i: (i, 0)),
            pl.BlockSpec((1, d, GLA_GATE_PAD), lambda i: (layer, 0, 0)),
            pl.BlockSpec((1, GLA_GATE_PAD, dk), lambda i: (layer, 0, 0)),
            pl.BlockSpec((1, 1, dk), lambda i: (layer, 0, 0)),
        ],
        out_specs=pl.BlockSpec((tm, dk), lambda i: (i, 0)),
        out_shape=jax.ShapeDtypeStruct((m, dk), F32),
        compiler_params=_params("parallel"),
        name="gla_gate",
    )(xn, wg1_bf, wgate_bf, b_gate)


def _gla_body(q_ref, k_ref, v_ref, r_ref, la_ref, s0_ref, gh_ref, *refs, chunk, q_scale):
    y_ref, sout_ref, s_scr = refs[-3:]
    t = pl.program_id(2)
    group, dkh, dvh = s_scr.shape
    tt = q_ref.shape[1]
    n = tt // chunk

    @pl.when(t == 0)
    def _():
        for g in range(group):
            s_scr[g] = s0_ref[0, 0, g]

    row = lax.broadcasted_iota(jnp.int32, (tt, tt), 0)
    col = lax.broadcasted_iota(jnp.int32, (tt, tt), 1)
    tri_sub = jnp.where((row >= col) & (row // chunk == col // chunk), 1.0, 0.0).astype(BF16)
    gh = gh_ref[0]
    ksl = [slice(g * dkh, (g + 1) * dkh) for g in range(group)]
    vsl = [slice(g * dvh, (g + 1) * dvh) for g in range(group)]
    rows = [slice(i * chunk, (i + 1) * chunk) for i in range(n)]

    def per_chunk(vecs):
        return jnp.concatenate([jnp.broadcast_to(v_, (chunk, v_.shape[1])) for v_ in vecs], axis=0)

    parts = _split3(la_ref[0])
    b = _dot(tri_sub, parts[0]) + _dot(tri_sub, parts[1]) + _dot(tri_sub, parts[2])
    total = [b[(i + 1) * chunk - 1:(i + 1) * chunk, :] for i in range(n)]
    start = [jnp.zeros_like(total[0])]
    for i in range(n):
        start.append(start[i] + total[i])

    qd = (q_ref[0] * q_scale) * jnp.exp(b)
    kf = k_ref[0]
    kd = kf * jnp.exp(per_chunk(total) - b)
    vb = v_ref[0].astype(BF16)
    q_dec = qd.astype(BF16)
    k_inv = (kf * jnp.exp(-b)).astype(BF16)
    q_state = (qd * per_chunk([jnp.exp(c_) for c_ in start[:n]])).astype(BF16)
    k_end = (kd * per_chunk([jnp.exp(start[n] - start[j + 1]) for j in range(n)])).astype(BF16)
    decay = jnp.exp(start[n])

    s = [s_scr[g] for g in range(group)]
    o_state = [_dot(q_state[:, ks], s_.astype(BF16)) for ks, s_ in zip(ksl, s)]
    upd = [_dot_tn(k_end[:, ks], vb[:, vs]) for ks, vs in zip(ksl, vsl)]
    for g in range(group):
        decay_col = jnp.transpose(jnp.broadcast_to(decay[:, ksl[g]], (LANES_V7X, dkh)))
        s_scr[g] = s[g] * jnp.tile(decay_col, (1, dvh // LANES_V7X)) + upd[g]

    keys = []
    for i in range(n):
        past = [(kd[rows[j], :] * jnp.exp(start[i] - start[j + 1])).astype(BF16) for j in range(i)]
        keys.append(jnp.concatenate(past + [k_inv[rows[i], :]], axis=0))
    scores = {}
    for i in range(n):
        width = (i + 1) * chunk
        visible = (lax.broadcasted_iota(jnp.int32, (chunk, width), 1)
                   <= lax.broadcasted_iota(jnp.int32, (chunk, width), 0) + i * chunk)
        for g in range(group):
            sc = _dot_nt(q_dec[rows[i], ksl[g]], keys[i][:, ksl[g]])
            scores[i, g] = jnp.where(visible, sc, 0.0).astype(BF16)
    intra = {(i, g): _dot(scores[i, g], vb[0:(i + 1) * chunk, vsl[g]]) for i in range(n) for g in range(group)}
    for g in range(group):
        o = o_state[g] + jnp.concatenate([intra[i, g] for i in range(n)], axis=0)
        ms = jnp.mean(o * o, axis=-1, keepdims=True)
        on = (o * lax.rsqrt(ms + RMS_EPS)) * gh
        rr = r_ref[0, :, vsl[g]]
        y_ref[0, :, vsl[g]] = (on * (rr * _sigmoid(rr))).astype(y_ref.dtype)

    @pl.when(t == pl.num_programs(2) - 1)
    def _():
        for g in range(group):
            sout_ref[0, 0, g] = s_scr[g]


def _gla_recurrence(proj, log_a, s0, s0_layer, g_head, layer, states, *, heads, dk, dv):
    bsz, seq, _ = proj.shape
    dkh, dvh = dk // heads, dv // heads
    group = GLA_HEAD_GROUP
    tt = _tile(seq, 256)
    assert tt % GLA_CHUNK == 0 and heads % group == 0 and (2 * dk) % (group * dvh) == 0
    hg = heads // group
    v_off = 2 * dk // (group * dvh)
    body = functools.partial(_gla_body, chunk=GLA_CHUNK, q_scale=dkh ** -0.5)
    return pl.pallas_call(
        body,
        grid=(bsz, hg, seq // tt),
        in_specs=[
            pl.BlockSpec((1, tt, group * dkh), lambda b, h, t: (b, t, h)),
            pl.BlockSpec((1, tt, group * dkh), lambda b, h, t: (b, t, hg + h)),
            pl.BlockSpec((1, tt, group * dvh), lambda b, h, t: (b, t, v_off + h)),
            pl.BlockSpec((1, tt, group * dvh), lambda b, h, t: (b, t, v_off + hg + h)),
            pl.BlockSpec((1, tt, group * dkh), lambda b, h, t: (b, t, h)),
            pl.BlockSpec((1, 1, group, dkh, dvh), lambda b, h, t: (s0_layer, b, h, 0, 0)),
            pl.BlockSpec((1, 1, dvh), lambda b, h, t: (layer, 0, 0)),
            pl.BlockSpec(memory_space=pl.ANY),
        ],
        out_specs=[
            pl.BlockSpec((1, tt, group * dvh), lambda b, h, t: (b, t, h)),
            pl.BlockSpec((1, 1, group, dkh, dvh), lambda b, h, t: (layer, b, h, 0, 0)),
        ],
        out_shape=[
            jax.ShapeDtypeStruct((bsz, seq, dv), BF16),
            jax.ShapeDtypeStruct(states.shape, F32),
        ],
        scratch_shapes=[pltpu.VMEM((group, dkh, dvh), F32)],
        input_output_aliases={7: 1},
        compiler_params=_params("parallel", "parallel", "arbitrary"),
        name="gla_recurrence",
    )(proj, proj, proj, proj, log_a, s0, g_head, states)


def _gla_mixer(xp, xs, hp, hs, state, new_p, new_s, layer, seq_p, seq_s, w_in, wg1_bf, wgate_bf, b_gate, g_head, w_out,
               *, heads, dk, dv):
    proj_p, proj_s = _matmul(xp, jnp.swapaxes(w_in, 1, 2), layer, n_out=2 * dk + 2 * dv, side=xs,
                             w_transposed=True, name="gla_in")

    def recur(xn, proj, seq, s0, s0_layer, new):
        bsz = xn.shape[0] // seq
        log_a = _gla_gate(xn, wg1_bf, wgate_bf, b_gate, layer).reshape(bsz, seq, dk)
        proj = proj.reshape(bsz, seq, -1)
        pad = (-seq) % GLA_CHUNK
        if pad:
            proj = jnp.pad(proj, ((0, 0), (0, pad), (0, 0)))
            log_a = jnp.pad(log_a, ((0, 0), (0, pad), (0, 0)))
        y, new = _gla_recurrence(proj, log_a, s0, s0_layer, g_head, layer, new, heads=heads, dk=dk, dv=dv)
        return y[:, :seq].reshape(xn.shape[0], dv), new

    zero_state = jnp.zeros((1, xp.shape[0] // seq_p) + state.shape[2:], F32)
    yp, sp = recur(xp, proj_p, seq_p, zero_state, 0, new_p)
    ys, ss = recur(xs, proj_s, seq_s, state, layer, new_s)
    hp, hs = _matmul(yp, w_out, layer, n_out=w_out.shape[2], res=hp, side=ys, side_res=hs, name="gla_out")
    return hp, hs, sp, ss


def _sb_prompt_body(bias_ref, q_ref, k_ref, v_ref, o_ref, kb_scr, vb_scr, *, scale):
    h = pl.program_id(1)
    i = pl.program_id(2)
    blk, rows = SB_BLOCK, SB_ROWS
    n_sub = blk // rows
    hd = q_ref.shape[2]

    @pl.when(i == 0)
    def _():
        kb_scr[...] = k_ref[0].astype(BF16)
        vb_scr[...] = v_ref[0].astype(BF16)

    bias = bias_ref[h]
    r_io = lax.broadcasted_iota(jnp.int32, (blk, blk), 0)
    c_io = lax.broadcasted_iota(jnp.int32, (blk, blk), 1)
    upper = jnp.where(r_io > c_io, -1.0, 0.0).astype(BF16)
    upper2 = jnp.concatenate([upper, upper], axis=0)
    sub_r = lax.broadcasted_iota(jnp.int32, (rows, blk), 0)
    sub_c = lax.broadcasted_iota(jnp.int32, (rows, blk), 1)

    def sweep(jobs, state):
        state = dict(state)
        keys = [pl.ds(pl.multiple_of(kb * blk, blk), blk) for _, kb, _ in jobs]
        z = [_dot_nt(q_ref[0, r0:r0 + rows, :], kb_scr[ks, :]) * scale + bias for (r0, _, _), ks in zip(jobs, keys)]
        sp = [_softplus(zz) for zz in z]
        vis = [sub_c < sub_r + (r0 % blk) if m else None for r0, _, m in jobs]
        nlk = [s_ if v_ is None else jnp.where(v_, s_, 0.0) for s_, v_ in zip(sp, vis)]
        later = [_dot(jnp.concatenate(_split2(l_), axis=1), upper2) for l_ in nlk]
        for (r0, _, _), ks, zz, s_, v_, l_, lt in zip(jobs, keys, z, sp, vis, nlk, later):
            tail, o = state[r0]
            a = jnp.exp((jnp.concatenate([tail] * (blk // LANES_V7X), axis=1) + lt) + (zz - s_))
            if v_ is not None:
                a = jnp.where(v_, a, 0.0)
            o = o + _dot(a.astype(BF16), vb_scr[ks, :])
            tail = tail + jnp.broadcast_to(lt[:, 0:1] - l_[:, 0:1], tail.shape)
            state[r0] = (tail, o)
        return state

    zero = (jnp.zeros((rows, LANES_V7X), F32), jnp.zeros((rows, hd), F32))
    rows_a = [s * rows for s in range(n_sub)]
    rows_b = [blk + s * rows for s in range(n_sub)]
    all_rows = rows_a + rows_b
    state = sweep([(r, 2 * i + 1, True) for r in rows_b] + [(r, 2 * i, True) for r in rows_a]
                  + [(r, 2 * i, False) for r in rows_b], {r: zero for r in all_rows})

    def step(jj, carry):
        kb = 2 * i - 1 - jj
        state = sweep([(r, kb, False) for r in all_rows], dict(zip(all_rows, carry)))
        return tuple(state[r] for r in all_rows)

    final = lax.fori_loop(0, 2 * i, step, tuple(state[r] for r in all_rows))
    for r, (_, o) in zip(all_rows, final):
        o_ref[0, r:r + rows, :] = o.astype(o_ref.dtype)


def _sb_prompt(q, k, v, bias, *, heads):
    bsz, seq, d = q.shape
    hd = d // heads
    tq = 2 * SB_BLOCK
    assert hd == LANES_V7X and seq % tq == 0
    body = functools.partial(_sb_prompt_body, scale=hd ** -0.5)
    return pl.pallas_call(
        body,
        grid=(bsz, heads, seq // tq),
        in_specs=[
            pl.BlockSpec(memory_space=pltpu.SMEM),
            pl.BlockSpec((1, tq, hd), lambda b, h, i: (b, i, h)),
            pl.BlockSpec((1, seq, hd), lambda b, h, i: (b, 0, h)),
            pl.BlockSpec((1, seq, hd), lambda b, h, i: (b, 0, h)),
        ],
        out_specs=pl.BlockSpec((1, tq, hd), lambda b, h, i: (b, i, h)),
        out_shape=jax.ShapeDtypeStruct((bsz, seq, d), BF16),
        scratch_shapes=[pltpu.VMEM((seq, hd), BF16), pltpu.VMEM((seq, hd), BF16)],
        compiler_params=_params("parallel", "parallel", "arbitrary"),
        name="sb_prompt",
    )(bias, q, k, v)


def _sb_sample_body(pt_ref, qbd_ref, bias_ref, kn_ref, vn_ref, *refs, heads, n_new, scale, pages_per_step):
    del pt_ref
    kp_refs, vp_refs = refs[:pages_per_step], refs[pages_per_step:2 * pages_per_step]
    o_ref, acc_scr, tail_scr = refs[2 * pages_per_step:]
    p = pl.program_id(1)
    page, nl = kn_ref.shape[1], qbd_ref.shape[2]
    lower = (lax.broadcasted_iota(jnp.int32, (page, page), 1) > lax.broadcasted_iota(jnp.int32, (page, page), 0)).astype(BF16)

    def blocks(kbs, vbs, vis):
        z = [_dot(kb, qbd_ref[0]) * scale + bias_ref[...] for kb in kbs]
        sp = [_softplus(zz) for zz in z]
        lk = [-s_ if vis is None else jnp.where(vis, -s_, 0.0) for s_ in sp]
        later = []
        for l_ in lk:
            lk_hi, lk_mid, lk_lo = _split3(l_)
            later.append(_dot(lower, lk_hi) + _dot(lower, lk_mid) + _dot(lower, lk_lo))
        tail = tail_scr[0:1, :]
        acc = acc_scr[...]
        for zz, s_, l_, lt, vb in zip(z, sp, lk, later, vbs):
            a = jnp.exp((tail + lt) + (zz - s_))
            if vis is not None:
                a = jnp.where(vis, a, 0.0)
            acc = acc + _dot(jnp.transpose(a).astype(BF16), vb)
            tail = tail + jnp.sum(l_, axis=0, keepdims=True)
        acc_scr[...] = acc
        tail_scr[0:1, :] = tail

    def heads_to_lanes(ref):
        by_head = jnp.swapaxes(ref[0, 0], 0, 1)
        return jnp.concatenate([by_head[h].astype(BF16) for h in range(heads)], axis=1)

    @pl.when(p == 0)
    def _():
        acc_scr[...] = jnp.zeros_like(acc_scr)
        tail_scr[...] = jnp.zeros_like(tail_scr)
        key = lax.broadcasted_iota(jnp.int32, (page, nl), 0)
        tok = lax.broadcasted_iota(jnp.int32, (page, nl), 1) // heads
        blocks([kn_ref[0].astype(BF16)], [vn_ref[0].astype(BF16)], (key < tok) & (key < n_new))

    @pl.when(p > 0)
    def _():
        blocks([heads_to_lanes(r) for r in kp_refs], [heads_to_lanes(r) for r in vp_refs], None)

    @pl.when(p == pl.num_programs(1) - 1)
    def _():
        d = acc_scr.shape[1]
        hd = d // heads
        own = lax.broadcasted_iota(jnp.int32, (heads, d), 0) == lax.broadcasted_iota(jnp.int32, (heads, d), 1) // hd
        for t in range(n_new):
            rows = acc_scr[t * heads:(t + 1) * heads, :]
            o_ref[0, t:t + 1, :] = jnp.sum(jnp.where(own, rows, 0.0), axis=0, keepdims=True).astype(o_ref.dtype)


def _sb_sample(q, k_new, v_new, cache_k, cache_v, layer, page_table, bias, *, heads):
    bd, t_new, d = q.shape
    hd = d // heads
    page = cache_k.shape[2]
    n_pages = page_table.shape[1]
    nl = LANES_V7X
    assert heads * t_new <= nl and heads % 8 == 0 and hd % LANES_V7X == 0 and t_new <= page
    q4 = q.reshape(bd, t_new, heads, hd)
    qbd = jnp.einsum("bthx,hg->bhxtg", q4, jnp.eye(heads, dtype=q.dtype)).reshape(bd, d, t_new * heads)
    qbd = jnp.pad(qbd, ((0, 0), (0, 0), (0, nl - t_new * heads))).astype(BF16)
    bias_l = jnp.pad(jnp.tile(bias, t_new), (0, nl - t_new * heads)).reshape(1, nl)
    kn = jnp.pad(k_new, ((0, 0), (0, page - t_new), (0, 0)))
    vn = jnp.pad(v_new, ((0, 0), (0, page - t_new), (0, 0)))

    pps = SB_PAGES_PER_STEP
    assert n_pages % pps == 0

    def page_map(slot):
        return lambda b, p, pt: (layer, pt[b, n_pages - pps * jnp.maximum(p, 1) + (pps - 1 - slot)], 0, 0, 0)

    page_specs = [pl.BlockSpec((1, 1, page, heads, hd), page_map(slot)) for slot in range(pps)]
    body = functools.partial(_sb_sample_body, heads=heads, n_new=t_new, scale=hd ** -0.5, pages_per_step=pps)
    grid_spec = pltpu.PrefetchScalarGridSpec(
        num_scalar_prefetch=1,
        grid=(bd, n_pages // pps + 1),
        in_specs=[
            pl.BlockSpec((1, d, nl), lambda b, p, pt: (b, 0, 0)),
            pl.BlockSpec((1, nl), lambda b, p, pt: (0, 0)),
            pl.BlockSpec((1, page, d), lambda b, p, pt: (b, 0, 0)),
            pl.BlockSpec((1, page, d), lambda b, p, pt: (b, 0, 0)),
        ] + page_specs + page_specs,
        out_specs=pl.BlockSpec((1, t_new, d), lambda b, p, pt: (b, 0, 0)),
        scratch_shapes=[pltpu.VMEM((nl, d), F32), pltpu.VMEM((8, nl), F32)],
    )
    return pl.pallas_call(
        body,
        grid_spec=grid_spec,
        out_shape=jax.ShapeDtypeStruct((bd, t_new, d), F32),
        compiler_params=_params("parallel", "arbitrary"),
        name="sb_sample",
    )(page_table, qbd, bias_l, kn, vn, *([cache_k] * pps), *([cache_v] * pps))


def _conv_prompt_body(x_ref, xs_ref, wb_ref, wc_ref, wh_ref, wconv_ref, buf0_ref, y_ref, st_ref, sb_ref, sc_ref, sh_ref,
                      u_scr, *, blocks_per_seq):
    m = pl.program_id(1)
    tm = x_ref.shape[0]
    x = x_ref[...]
    wb, wc, wh = wb_ref[0].astype(BF16), wc_ref[0].astype(BF16), wh_ref[0].astype(BF16)
    bg = _dot(x, wb)
    u = _dot(x, wc) * _dot(x, wh)

    @pl.when(m % blocks_per_seq == 0)
    def _():
        u_scr[6:8, :] = buf0_ref[0]

    @pl.when(m == 0)
    def _():
        xs = xs_ref[...]
        sb_ref[...] = _dot(xs, wb)
        sc_ref[...] = _dot(xs, wc)
        sh_ref[...] = _dot(xs, wh)

    u_scr[8:8 + tm, :] = u
    wconv = wconv_ref[0]
    conv = wconv[0:1, :] * u_scr[6:6 + tm, :] + wconv[1:2, :] * u_scr[7:7 + tm, :] + wconv[2:3, :] * u
    y_ref[...] = (bg * conv).astype(y_ref.dtype)
    st_ref[0] = u_scr[tm + 6:tm + 8, :]
    u_scr[0:8, :] = u_scr[tm:tm + 8, :]


def _conv_prompt(xn, side, w_in, w_conv, layer, buf0, seq):
    m, d = xn.shape
    ms = side.shape[0]
    tm = _tile(seq, 1024)
    tn = _tile(d, 256)
    nb = d // tn
    blocks_per_seq = seq // tm
    body = functools.partial(_conv_prompt_body, blocks_per_seq=blocks_per_seq)
    side_spec = pl.BlockSpec((ms, tn), lambda n, i: (0, n))
    side_shape = jax.ShapeDtypeStruct((ms, d), F32)
    return pl.pallas_call(
        body,
        grid=(nb, m // tm),
        in_specs=[
            pl.BlockSpec((tm, d), lambda n, i: (i, 0)),
            pl.BlockSpec((ms, d), lambda n, i: (0, 0)),
            pl.BlockSpec((1, d, tn), lambda n, i: (layer, 0, n)),
            pl.BlockSpec((1, d, tn), lambda n, i: (layer, 0, nb + n)),
            pl.BlockSpec((1, d, tn), lambda n, i: (layer, 0, 2 * nb + n)),
            pl.BlockSpec((1, w_conv.shape[1], tn), lambda n, i: (layer, 0, n)),
            pl.BlockSpec((1, 2, tn), lambda n, i: (i // blocks_per_seq, 0, n)),
        ],
        out_specs=[
            pl.BlockSpec((tm, tn), lambda n, i: (i, n)),
            pl.BlockSpec((1, 2, tn), lambda n, i: (i // blocks_per_seq, 0, n)),
            side_spec, side_spec, side_spec,
        ],
        out_shape=[
            jax.ShapeDtypeStruct((m, d), BF16),
            jax.ShapeDtypeStruct((m // seq, 2, d), F32),
            side_shape, side_shape, side_shape,
        ],
        scratch_shapes=[pltpu.VMEM((tm + 8, tn), F32)],
        compiler_params=_params("arbitrary", "arbitrary"),
        name="conv_prompt",
    )(xn, side, w_in, w_in, w_in, w_conv, buf0)


def _conv_sample_body(bg_ref, cg_ref, h_ref, buf0_ref, wconv_ref, y_ref, st_ref, *, seq):
    taps = [buf0_ref[0], buf0_ref[1]]
    for t in range(seq):
        taps.append(cg_ref[t] * h_ref[t])
    for t in range(seq):
        conv = wconv_ref[0:1, :] * taps[t] + wconv_ref[1:2, :] * taps[t + 1] + wconv_ref[2:3, :] * taps[t + 2]
        y_ref[t] = (bg_ref[t] * conv).astype(y_ref.dtype)
    st_ref[0] = taps[seq]
    st_ref[1] = taps[seq + 1]


def _conv_sample(bg_tm, cg_tm, h_tm, buf0_tm, w_conv):
    seq, bd, d = bg_tm.shape
    return pl.pallas_call(
        functools.partial(_conv_sample_body, seq=seq),
        out_shape=[jax.ShapeDtypeStruct((seq, bd, d), BF16), jax.ShapeDtypeStruct((2, bd, d), F32)],
        compiler_params=pltpu.CompilerParams(vmem_limit_bytes=VMEM_LIMIT_V7X),
        name="conv_sample",
    )(bg_tm, cg_tm, h_tm, buf0_tm, w_conv)


def kernel(x_prompt, x_sample, state_gla, cache_sb_k, cache_sb_v, state_conv, cache_mem_k, cache_mem_v, page_table, mem_prompt, g_mix, g_xattn, g_mem, g_ffn, g_final, w_gla_in, w_gla_gate, b_gla_gate, g_gla_head, w_gla_out, w_sb_in, b_sb, w_sb_out, w_conv_in, w_conv, w_conv_out, w_xq, w_xkv, w_xo, w_ffn_in, w_ffn_out):
    bsz, seq, d = x_prompt.shape
    bd, t_new, _ = x_sample.shape
    depth = g_mix.shape[0]
    n_mem = mem_prompt.shape[1]
    gla_heads = state_gla.shape[2]
    dk = w_gla_gate.shape[2]
    dv = w_gla_out.shape[1]
    sb_heads = b_sb.shape[1]
    x_heads = cache_mem_k.shape[3]
    xw = w_xq.shape[2]
    mp, ms = bsz * seq, bd * t_new

    bf = lambda w: w.astype(BF16)
    rank = w_gla_gate.shape[1]
    w_gla_g1 = bf(jnp.pad(w_gla_in[:, :, 2 * dk + 2 * dv:], ((0, 0), (0, 0), (0, GLA_GATE_PAD - rank))))
    w_gla_gate_p = bf(jnp.pad(w_gla_gate, ((0, 0), (0, GLA_GATE_PAD - rank), (0, 0))))
    w_xq_bf, w_xkv_bf, w_xo_bf = bf(w_xq), bf(w_xkv), bf(w_xo)
    b_gla_gate3 = b_gla_gate.reshape(-1, 1, dk)
    g_gla_head3 = g_gla_head.reshape(-1, 1, dv // gla_heads)
    g_xattn3 = g_xattn.reshape(depth, 1, d)
    g_ffn3 = g_ffn.reshape(depth, 1, d)
    cache_mem_k3 = cache_mem_k.reshape(depth, bd * n_mem, xw)
    cache_mem_v3 = cache_mem_v.reshape(depth, bd * n_mem, xw)

    hp = x_prompt.reshape(mp, d)
    hs = x_sample.reshape(ms, d)
    mem_k, mem_v = _mem_kv(mem_prompt.reshape(bsz * n_mem, d), g_mem, w_xkv_bf)

    gla_p = jnp.zeros((state_gla.shape[0], bsz) + state_gla.shape[2:], F32)
    gla_s = jnp.zeros(state_gla.shape, F32)
    sbk_p, sbv_p, sbk_s, sbv_s, conv_p, conv_s = [], [], [], [], [], []
    for i in range(depth):
        kind, j = i % 3, i // 3
        xp = _rmsnorm(hp, g_mix[i], BF16)
        xs = _rmsnorm(hs, g_mix[i], BF16)
        if kind == 0:
            hp, hs, gla_p, gla_s = _gla_mixer(xp, xs, hp, hs, state_gla, gla_p, gla_s, j, seq, t_new, w_gla_in, w_gla_g1,
                                              w_gla_gate_p, b_gla_gate3, g_gla_head3, w_gla_out,
                                              heads=gla_heads, dk=dk, dv=dv)
        elif kind == 1:
            qp, qs = _matmul(xp, w_sb_in, j, n_out=d, col_off=0, out_dtype=BF16, side=xs, name="sb_q")
            kp, kn = _matmul(xp, w_sb_in, j, n_out=d, col_off=d, side=xs, name="sb_k")
            vp, vn = _matmul(xp, w_sb_in, j, n_out=d, col_off=2 * d, side=xs, name="sb_v")
            qs, kn, vn = (a.reshape(bd, t_new, d) for a in (qs, kn, vn))
            op = _sb_prompt(qp.reshape(bsz, seq, d), kp.reshape(bsz, seq, d), vp.reshape(bsz, seq, d), b_sb[j],
                            heads=sb_heads)
            os_ = _sb_sample(qs, kn, vn, cache_sb_k, cache_sb_v, j, page_table, b_sb[j], heads=sb_heads)
            hp, hs = _matmul(op.reshape(mp, d), w_sb_out, j, n_out=d, res=hp,
                             side=os_.reshape(ms, d).astype(BF16), side_res=hs, name="sb_out")
            sbk_p.append(kp)
            sbv_p.append(vp)
            sbk_s.append(kn)
            sbv_s.append(vn)
        else:
            yp, bp, *side_proj = _conv_prompt(xp, xs, w_conv_in, w_conv, j, jnp.zeros((bsz, 2, d), F32), seq)
            ys, bs = _conv_sample(*(a.reshape(bd, t_new, d).transpose(1, 0, 2) for a in side_proj),
                                  state_conv[j].transpose(1, 0, 2), w_conv[j])
            hp, hs = _matmul(yp, w_conv_out, j, n_out=d, res=hp,
                             side=ys.transpose(1, 0, 2).reshape(ms, d), side_res=hs, name="conv_out")
            conv_p.append(bp)
            conv_s.append(bs.transpose(1, 0, 2))
        hp, xp_ffn = _xattn(hp, g_xattn3, g_ffn3, w_xq_bf, mem_k, mem_v, w_xo_bf, i,
                            heads=x_heads, rows_per_seq=seq, n_mem=n_mem, grouped=False)
        hs, xs_ffn = _xattn(hs, g_xattn3, g_ffn3, w_xq_bf, cache_mem_k3, cache_mem_v3, w_xo_bf, i,
                            heads=x_heads, rows_per_seq=t_new, n_mem=n_mem, grouped=True)
        ap, as_, w_out_bf = _swiglu_in(xp_ffn, xs_ffn, w_ffn_in, w_ffn_out, i, tm_pref=2048, tf_pref=256)
        hp, hs = _matmul(ap, w_out_bf[None], 0, n_out=d, res=hp, side=as_, side_res=hs,
                         tm_pref=512, tn_pref=512, single_buffer_x=False, cols_outer=True, name="ffn_out")

    y_prompt = _rmsnorm(hp, g_final, F32).reshape(bsz, seq, d)
    y_sample = _rmsnorm(hs, g_final, F32).reshape(bd, t_new, d)
    hd = d // sb_heads
    sb_p = lambda xs_: jnp.stack(xs_).reshape(-1, bsz, seq, sb_heads, hd)
    sb_s = lambda xs_: jnp.stack(xs_).reshape(-1, bd, t_new, sb_heads, hd)
    xhd = xw // x_heads
    return (y_prompt, y_sample,
            gla_p, gla_s,
            sb_p(sbk_p), sb_p(sbv_p), sb_s(sbk_s), sb_s(sbv_s),
            jnp.stack(conv_p), jnp.stack(conv_s),
            mem_k.reshape(depth, bsz, n_mem, x_heads, xhd), mem_v.reshape(depth, bsz, n_mem, x_heads, xhd))
```

```python
import functools

import jax
import jax.numpy as jnp
from jax import lax
from jax.experimental import pallas as pl
from jax.experimental.pallas import tpu as pltpu

F32 = jnp.float32
BF16 = jnp.bfloat16

LANES_V7X = 128
VMEM_LIMIT_V7X = 56 * 1024 * 1024
RMS_EPS = 1e-6
GLA_TAU = 16.0
GLA_CHUNK = 32
GLA_GATE_PAD = LANES_V7X
GLA_HEAD_GROUP = 2
SB_BLOCK = 256
SB_ROWS = 128
SB_PAGES_PER_STEP = 4
NEG_BIG = -1e30
LOG2E = 1.4426950408889634


def _softplus_bits(z2):
    return jnp.maximum(z2, 0.0) + jnp.log(1.0 + jnp.exp2(-jnp.abs(z2))) * LOG2E


def _params(*sem):
    return pltpu.CompilerParams(dimension_semantics=sem, vmem_limit_bytes=VMEM_LIMIT_V7X)


def _tile(n, pref):
    if n <= pref:
        return n
    t = (pref // LANES_V7X) * LANES_V7X
    while t > LANES_V7X and n % t:
        t -= LANES_V7X
    assert n % t == 0, (n, pref)
    return t


def _softplus(z):
    return jnp.maximum(z, 0.0) + jnp.log(1.0 + jnp.exp(-jnp.abs(z)))


def _sigmoid(z):
    return 1.0 / (1.0 + jnp.exp(-z))


def _split2(x):
    hi = x.astype(BF16)
    lo = (x - hi.astype(F32)).astype(BF16)
    return hi, lo


def _split3(x):
    hi = x.astype(BF16)
    r1 = x - hi.astype(F32)
    mid = r1.astype(BF16)
    lo = (r1 - mid.astype(F32)).astype(BF16)
    return hi, mid, lo


def _dot(a, b):
    return jnp.dot(a, b, preferred_element_type=F32)


def _dot_nt(a, b):
    return lax.dot_general(a, b, (((1,), (1,)), ((), ())), preferred_element_type=F32)


def _dot_tn(a, b):
    return lax.dot_general(a, b, (((0,), (0,)), ((), ())), preferred_element_type=F32)


def _rmsnorm_body(x_ref, g_ref, o_ref):
    x = x_ref[...]
    ms = jnp.mean(x * x, axis=-1, keepdims=True)
    o_ref[...] = ((x * lax.rsqrt(ms + RMS_EPS)) * g_ref[...]).astype(o_ref.dtype)


def _rmsnorm(x, g, out_dtype):
    m, d = x.shape
    tm = _tile(m, 512)
    return pl.pallas_call(
        _rmsnorm_body,
        grid=(m // tm,),
        in_specs=[pl.BlockSpec((tm, d), lambda i: (i, 0)), pl.BlockSpec((1, d), lambda i: (0, 0))],
        out_specs=pl.BlockSpec((tm, d), lambda i: (i, 0)),
        out_shape=jax.ShapeDtypeStruct((m, d), out_dtype),
        compiler_params=_params("parallel"),
        name="rmsnorm",
    )(x, g.reshape(1, d))


def _mm_body(*refs, has_res, has_side, has_side_res, w_transposed, row_axis):
    dot = _dot_nt if w_transposed else _dot
    it = iter(refs)
    x_ref, w_ref = next(it), next(it)
    r_ref = next(it) if has_res else None
    xs_ref = next(it) if has_side else None
    rs_ref = next(it) if has_side_res else None
    o_ref = next(it)
    os_ref = next(it) if has_side else None
    wb = w_ref[0].astype(BF16)
    acc = dot(x_ref[...], wb)
    if has_res:
        acc = acc + r_ref[...]
    o_ref[...] = acc.astype(o_ref.dtype)
    if has_side:
        @pl.when(pl.program_id(row_axis) == 0)
        def _():
            side = dot(xs_ref[...], wb)
            if has_side_res:
                side = side + rs_ref[...]
            os_ref[...] = side.astype(os_ref.dtype)


def _side_col(i, j, last):
    return jnp.where(i == 0, j, last)


def _matmul(x, w, layer, *, n_out, col_off=0, res=None, out_dtype=F32, side=None, side_res=None, side_dtype=F32,
            tm_pref=1024, tn_pref=512, single_buffer_x=False, w_transposed=False, cols_outer=False, name="matmul"):
    m, k = x.shape
    tm, tn = _tile(m, tm_pref), _tile(n_out, tn_pref)
    assert col_off % tn == 0
    off = col_off // tn
    nj = n_out // tn
    if cols_outer:
        grid = (nj, m // tm)
        ij = lambda a, b: (b, a)
        side_col = lambda a, b: a
        w_mode = dict(pipeline_mode=pl.Buffered(1))
    else:
        grid = (m // tm, nj)
        ij = lambda a, b: (a, b)
        side_col = lambda a, b: _side_col(a, b, nj - 1)
        w_mode = {}
    x_mode = dict(pipeline_mode=pl.Buffered(1)) if single_buffer_x else {}
    w_spec = (pl.BlockSpec((1, tn, k), lambda a, b: (layer, ij(a, b)[1] + off, 0), **w_mode) if w_transposed else
              pl.BlockSpec((1, k, tn), lambda a, b: (layer, 0, ij(a, b)[1] + off), **w_mode))
    in_specs = [pl.BlockSpec((tm, k), lambda a, b: (ij(a, b)[0], 0), **x_mode), w_spec]
    args = [x, w]
    out_specs = [pl.BlockSpec((tm, tn), lambda a, b: ij(a, b))]
    out_shape = [jax.ShapeDtypeStruct((m, n_out), out_dtype)]
    if res is not None:
        in_specs.append(pl.BlockSpec((tm, tn), lambda a, b: ij(a, b)))
        args.append(res)
    if side is not None:
        ms = side.shape[0]
        in_specs.append(pl.BlockSpec((ms, k), lambda a, b: (0, 0)))
        args.append(side)
        if side_res is not None:
            in_specs.append(pl.BlockSpec((ms, tn), lambda a, b: (0, side_col(a, b))))
            args.append(side_res)
        out_specs.append(pl.BlockSpec((ms, tn), lambda a, b: (0, side_col(a, b))))
        out_shape.append(jax.ShapeDtypeStruct((ms, n_out), side_dtype))
    body = functools.partial(_mm_body, has_res=res is not None, has_side=side is not None,
                             has_side_res=side_res is not None, w_transposed=w_transposed,
                             row_axis=1 if cols_outer else 0)
    outs = pl.pallas_call(
        body,
        grid=grid,
        in_specs=in_specs,
        out_specs=out_specs,
        out_shape=out_shape,
        compiler_params=_params("arbitrary", "arbitrary"),
        name=name,
    )(*args)
    return outs if side is not None else outs[0]


def _mem_kv_body(x_ref, g_ref, w_ref, k_ref, v_ref):
    x = x_ref[...]
    ms = jnp.mean(x * x, axis=-1, keepdims=True)
    xn = ((x * lax.rsqrt(ms + RMS_EPS)) * g_ref[0]).astype(BF16)
    kv = _dot(xn, w_ref[0])
    half = kv.shape[1] // 2
    k_ref[0] = kv[:, :half]
    v_ref[0] = kv[:, half:]


def _mem_kv(mem, g_mem, w_xkv_bf):
    rows, d = mem.shape
    depth, _, two_w = w_xkv_bf.shape
    xw = two_w // 2
    tm = _tile(rows, 512)
    out = jax.ShapeDtypeStruct((depth, rows, xw), F32)
    return pl.pallas_call(
        _mem_kv_body,
        grid=(depth, rows // tm),
        in_specs=[
            pl.BlockSpec((tm, d), lambda i, m: (m, 0)),
            pl.BlockSpec((1, 1, d), lambda i, m: (i, 0, 0)),
            pl.BlockSpec((1, d, two_w), lambda i, m: (i, 0, 0)),
        ],
        out_specs=[pl.BlockSpec((1, tm, xw), lambda i, m: (i, m, 0))] * 2,
        out_shape=[out, out],
        compiler_params=_params("parallel", "parallel"),
        name="mem_kv",
    )(mem, g_mem.reshape(depth, 1, d), w_xkv_bf)


def _xattn_body(x_ref, g_ref, g_next_ref, wq_ref, mk_ref, mv_ref, wo_ref, o_ref, on_ref, *, heads, scale,
                rows_per_group, mem_per_group):
    x = x_ref[...]
    ms = jnp.mean(x * x, axis=-1, keepdims=True)
    xn = ((x * lax.rsqrt(ms + RMS_EPS)) * g_ref[0]).astype(BF16)
    q = _dot(xn, wq_ref[0])
    mk = mk_ref[0].astype(BF16)
    mv = mv_ref[0].astype(BF16)
    hd = q.shape[1] // heads
    outs = []
    for h in range(heads):
        sl = slice(h * hd, (h + 1) * hd)
        s = _dot_nt(q[:, sl].astype(BF16), mk[:, sl]) * scale
        if rows_per_group is not None:
            rg = lax.broadcasted_iota(jnp.int32, s.shape, 0) // rows_per_group
            cg = lax.broadcasted_iota(jnp.int32, s.shape, 1) // mem_per_group
            s = jnp.where(rg == cg, s, NEG_BIG)
        e = jnp.exp(s - jnp.max(s, axis=-1, keepdims=True))
        p = e / jnp.sum(e, axis=-1, keepdims=True)
        outs.append(_dot(p.astype(BF16), mv[:, sl]))
    o = jnp.concatenate(outs, axis=-1).astype(BF16)
    y = _dot(o, wo_ref[0]) + x
    o_ref[...] = y
    ms_y = jnp.mean(y * y, axis=-1, keepdims=True)
    on_ref[...] = ((y * lax.rsqrt(ms_y + RMS_EPS)) * g_next_ref[0]).astype(on_ref.dtype)


def _xattn(h, g, g_next, wq_bf, mk, mv, wo_bf, layer, *, heads, rows_per_seq, n_mem, grouped):
    m, d = h.shape
    xw = wq_bf.shape[2]
    scale = (xw // heads) ** -0.5
    if grouped:
        tm, mem_rows = m, mk.shape[1]
        mem_map = lambda i: (layer, 0, 0)
        rpg = rows_per_seq
    else:
        tm, mem_rows = _tile(rows_per_seq, 256), n_mem
        blocks_per_seq = rows_per_seq // tm
        mem_map = lambda i: (layer, i // blocks_per_seq, 0)
        rpg = None
    body = functools.partial(_xattn_body, heads=heads, scale=scale, rows_per_group=rpg, mem_per_group=n_mem)
    return pl.pallas_call(
        body,
        grid=(m // tm,),
        in_specs=[
            pl.BlockSpec((tm, d), lambda i: (i, 0)),
            pl.BlockSpec((1, 1, d), lambda i: (layer, 0, 0)),
            pl.BlockSpec((1, 1, d), lambda i: (layer, 0, 0)),
            pl.BlockSpec((1, d, xw), lambda i: (layer, 0, 0)),
            pl.BlockSpec((1, mem_rows, xw), mem_map),
            pl.BlockSpec((1, mem_rows, xw), mem_map),
            pl.BlockSpec((1, xw, d), lambda i: (layer, 0, 0)),
        ],
        out_specs=[pl.BlockSpec((tm, d), lambda i: (i, 0))] * 2,
        out_shape=[jax.ShapeDtypeStruct((m, d), F32), jax.ShapeDtypeStruct((m, d), BF16)],
        compiler_params=_params("parallel"),
        name="xattn",
    )(h, g, g_next, wq_bf, mk, mv, wo_bf)


def _swiglu_in_body(x_ref, xs_ref, wg_ref, wu_ref, wnext_ref, o_ref, os_ref, wnext_bf_ref):
    wg = wg_ref[0].astype(BF16)
    wu = wu_ref[0].astype(BF16)

    def act(x):
        gate = _dot(x, wg)
        return (gate * _sigmoid(gate)) * _dot(x, wu)

    o_ref[...] = act(x_ref[...]).astype(o_ref.dtype)

    @pl.when(pl.program_id(0) == 0)
    def _():
        os_ref[...] = act(xs_ref[...]).astype(os_ref.dtype)
        wnext_bf_ref[...] = wnext_ref[0].astype(BF16)


def _swiglu_in(xn, side, w_in, w_next, layer, *, tm_pref, tf_pref):
    m, d = xn.shape
    ms = side.shape[0]
    f = w_in.shape[2] // 2
    tm, tf = _tile(m, tm_pref), _tile(f, tf_pref)
    nf = f // tf
    d_next = w_next.shape[2]
    return pl.pallas_call(
        _swiglu_in_body,
        grid=(m // tm, nf),
        in_specs=[
            pl.BlockSpec((tm, d), lambda i, j: (i, 0), pipeline_mode=pl.Buffered(1)),
            pl.BlockSpec((ms, d), lambda i, j: (0, 0)),
            pl.BlockSpec((1, d, tf), lambda i, j: (layer, 0, j)),
            pl.BlockSpec((1, d, tf), lambda i, j: (layer, 0, j + nf)),
            pl.BlockSpec((1, tf, d_next), lambda i, j: (layer, _side_col(i, j, nf - 1), 0)),
        ],
        out_specs=[pl.BlockSpec((tm, tf), lambda i, j: (i, j)),
                   pl.BlockSpec((ms, tf), lambda i, j: (0, _side_col(i, j, nf - 1))),
                   pl.BlockSpec((tf, d_next), lambda i, j: (_side_col(i, j, nf - 1), 0))],
        out_shape=[jax.ShapeDtypeStruct((m, f), BF16), jax.ShapeDtypeStruct((ms, f), BF16),
                   jax.ShapeDtypeStruct((f, d_next), BF16)],
        compiler_params=_params("arbitrary", "arbitrary"),
        name="swiglu_in",
    )(xn, side, w_in, w_in, w_next)


def _gla_gate_body(x_ref, wg1_ref, wgate_ref, b_ref, o_ref):
    g1 = _dot(x_ref[...], wg1_ref[0])
    z = _dot(g1.astype(BF16), wgate_ref[0]) + b_ref[0]
    log_sig = jnp.minimum(z, 0.0) - jnp.log1p(jnp.exp(-jnp.abs(z)))
    o_ref[...] = log_sig / GLA_TAU


def _gla_gate(xn, wg1_bf, wgate_bf, b_gate, layer):
    m, d = xn.shape
    dk = wgate_bf.shape[2]
    tm = _tile(m, 512)
    return pl.pallas_call(
        _gla_gate_body,
        grid=(m // tm,),
        in_specs=[
            pl.BlockSpec((tm, d), lambda i: (i, 0)),
            pl.BlockSpec((1, d, GLA_GATE_PAD), lambda i: (layer, 0, 0)),
            pl.BlockSpec((1, GLA_GATE_PAD, dk), lambda i: (layer, 0, 0)),
            pl.BlockSpec((1, 1, dk), lambda i: (layer, 0, 0)),
        ],
        out_specs=pl.BlockSpec((tm, dk), lambda i: (i, 0)),
        out_shape=jax.ShapeDtypeStruct((m, dk), F32),
        compiler_params=_params("parallel"),
        name="gla_gate",
    )(xn, wg1_bf, wgate_bf, b_gate)


def _gla_body(q_ref, k_ref, v_ref, r_ref, la_ref, s0_ref, gh_ref, *refs, chunk, q_scale):
    y_ref, sout_ref, s_scr = refs[-3:]
    t = pl.program_id(2)
    group, dkh, dvh = s_scr.shape
    tt = q_ref.shape[1]
    n = tt // chunk

    @pl.when(t == 0)
    def _():
        for g in range(group):
            s_scr[g] = s0_ref[0, 0, g]

    row = lax.broadcasted_iota(jnp.int32, (tt, tt), 0)
    col = lax.broadcasted_iota(jnp.int32, (tt, tt), 1)
    tri_sub = jnp.where((row >= col) & (row // chunk == col // chunk), 1.0, 0.0).astype(BF16)
    gh = gh_ref[0]
    ksl = [slice(g * dkh, (g + 1) * dkh) for g in range(group)]
    vsl = [slice(g * dvh, (g + 1) * dvh) for g in range(group)]
    rows = [slice(i * chunk, (i + 1) * chunk) for i in range(n)]

    def per_chunk(vecs):
        return jnp.concatenate([jnp.broadcast_to(v_, (chunk, v_.shape[1])) for v_ in vecs], axis=0)

    parts = _split3(la_ref[0])
    b = _dot(tri_sub, parts[0]) + _dot(tri_sub, parts[1]) + _dot(tri_sub, parts[2])
    total = [b[(i + 1) * chunk - 1:(i + 1) * chunk, :] for i in range(n)]
    start = [jnp.zeros_like(total[0])]
    for i in range(n):
        start.append(start[i] + total[i])

    qd = (q_ref[0] * q_scale) * jnp.exp(b)
    kf = k_ref[0]
    kd = kf * jnp.exp(per_chunk(total) - b)
    vb = v_ref[0].astype(BF16)
    q_dec = qd.astype(BF16)
    k_inv = (kf * jnp.exp(-b)).astype(BF16)
    q_state = (qd * per_chunk([jnp.exp(c_) for c_ in start[:n]])).astype(BF16)
    k_end = (kd * per_chunk([jnp.exp(start[n] - start[j + 1]) for j in range(n)])).astype(BF16)
    decay = jnp.exp(start[n])

    s = [s_scr[g] for g in range(group)]
    o_state = [_dot(q_state[:, ks], s_.astype(BF16)) for ks, s_ in zip(ksl, s)]
    upd = [_dot_tn(k_end[:, ks], vb[:, vs]) for ks, vs in zip(ksl, vsl)]
    for g in range(group):
        decay_col = jnp.transpose(jnp.broadcast_to(decay[:, ksl[g]], (LANES_V7X, dkh)))
        s_scr[g] = s[g] * jnp.tile(decay_col, (1, dvh // LANES_V7X)) + upd[g]

    keys = []
    for i in range(n):
        past = [(kd[rows[j], :] * jnp.exp(start[i] - start[j + 1])).astype(BF16) for j in range(i)]
        keys.append(jnp.concatenate(past + [k_inv[rows[i], :]], axis=0))
    scores = {}
    for i in range(n):
        width = (i + 1) * chunk
        visible = (lax.broadcasted_iota(jnp.int32, (chunk, width), 1)
                   <= lax.broadcasted_iota(jnp.int32, (chunk, width), 0) + i * chunk)
        for g in range(group):
            sc = _dot_nt(q_dec[rows[i], ksl[g]], keys[i][:, ksl[g]])
            scores[i, g] = jnp.where(visible, sc, 0.0).astype(BF16)
    intra = {(i, g): _dot(scores[i, g], vb[0:(i + 1) * chunk, vsl[g]]) for i in range(n) for g in range(group)}
    for g in range(group):
        o = o_state[g] + jnp.concatenate([intra[i, g] for i in range(n)], axis=0)
        ms = jnp.mean(o * o, axis=-1, keepdims=True)
        on = (o * lax.rsqrt(ms + RMS_EPS)) * gh
        rr = r_ref[0, :, vsl[g]]
        y_ref[0, :, vsl[g]] = (on * (rr * _sigmoid(rr))).astype(y_ref.dtype)

    @pl.when(t == pl.num_programs(2) - 1)
    def _():
        for g in range(group):
            sout_ref[0, 0, g] = s_scr[g]


def _gla_recurrence(proj, log_a, s0, s0_layer, g_head, layer, states, *, heads, dk, dv):
    bsz, seq, _ = proj.shape
    dkh, dvh = dk // heads, dv // heads
    group = GLA_HEAD_GROUP
    tt = _tile(seq, 256)
    assert tt % GLA_CHUNK == 0 and heads % group == 0 and (2 * dk) % (group * dvh) == 0
    hg = heads // group
    v_off = 2 * dk // (group * dvh)
    body = functools.partial(_gla_body, chunk=GLA_CHUNK, q_scale=dkh ** -0.5)
    return pl.pallas_call(
        body,
        grid=(bsz, hg, seq // tt),
        in_specs=[
            pl.BlockSpec((1, tt, group * dkh), lambda b, h, t: (b, t, h)),
            pl.BlockSpec((1, tt, group * dkh), lambda b, h, t: (b, t, hg + h)),
            pl.BlockSpec((1, tt, group * dvh), lambda b, h, t: (b, t, v_off + h)),
            pl.BlockSpec((1, tt, group * dvh), lambda b, h, t: (b, t, v_off + hg + h)),
            pl.BlockSpec((1, tt, group * dkh), lambda b, h, t: (b, t, h)),
            pl.BlockSpec((1, 1, group, dkh, dvh), lambda b, h, t: (s0_layer, b, h, 0, 0)),
            pl.BlockSpec((1, 1, dvh), lambda b, h, t: (layer, 0, 0)),
            pl.BlockSpec(memory_space=pl.ANY),
        ],
        out_specs=[
            pl.BlockSpec((1, tt, group * dvh), lambda b, h, t: (b, t, h)),
            pl.BlockSpec((1, 1, group, dkh, dvh), lambda b, h, t: (layer, b, h, 0, 0)),
        ],
        out_shape=[
            jax.ShapeDtypeStruct((bsz, seq, dv), BF16),
            jax.ShapeDtypeStruct(states.shape, F32),
        ],
        scratch_shapes=[pltpu.VMEM((group, dkh, dvh), F32)],
        input_output_aliases={7: 1},
        compiler_params=_params("parallel", "parallel", "arbitrary"),
        name="gla_recurrence",
    )(proj, proj, proj, proj, log_a, s0, g_head, states)


def _gla_mixer(xp, xs, hp, hs, state, new_p, new_s, layer, seq_p, seq_s, w_in, wg1_bf, wgate_bf, b_gate, g_head, w_out,
               *, heads, dk, dv):
    proj_p, proj_s = _matmul(xp, jnp.swapaxes(w_in, 1, 2), layer, n_out=2 * dk + 2 * dv, side=xs,
                             w_transposed=True, name="gla_in")

    def recur(xn, proj, seq, s0, s0_layer, new):
        bsz = xn.shape[0] // seq
        log_a = _gla_gate(xn, wg1_bf, wgate_bf, b_gate, layer).reshape(bsz, seq, dk)
        proj = proj.reshape(bsz, seq, -1)
        pad = (-seq) % GLA_CHUNK
        if pad:
            proj = jnp.pad(proj, ((0, 0), (0, pad), (0, 0)))
            log_a = jnp.pad(log_a, ((0, 0), (0, pad), (0, 0)))
        y, new = _gla_recurrence(proj, log_a, s0, s0_layer, g_head, layer, new, heads=heads, dk=dk, dv=dv)
        return y[:, :seq].reshape(xn.shape[0], dv), new

    zero_state = jnp.zeros((1, xp.shape[0] // seq_p) + state.shape[2:], F32)
    yp, sp = recur(xp, proj_p, seq_p, zero_state, 0, new_p)
    ys, ss = recur(xs, proj_s, seq_s, state, layer, new_s)
    hp, hs = _matmul(yp, w_out, layer, n_out=w_out.shape[2], res=hp, side=ys, side_res=hs, name="gla_out")
    return hp, hs, sp, ss


def _sb_prompt_body(bias_ref, q_ref, k_ref, v_ref, o_ref, kb_scr, vb_scr, *, scale):
    h = pl.program_id(1)
    i = pl.program_id(2)
    blk, rows = SB_BLOCK, SB_ROWS
    n_sub = blk // rows
    hd = q_ref.shape[2]

    @pl.when(i == 0)
    def _():
        kb_scr[...] = k_ref[0].astype(BF16)
        vb_scr[...] = v_ref[0].astype(BF16)

    bias = bias_ref[h] * LOG2E
    scale = scale * LOG2E
    r_io = lax.broadcasted_iota(jnp.int32, (blk, blk), 0)
    c_io = lax.broadcasted_iota(jnp.int32, (blk, blk), 1)
    upper = jnp.where(r_io > c_io, -1.0, 0.0).astype(BF16)
    upper2 = jnp.concatenate([upper, upper], axis=0)
    sub_r = lax.broadcasted_iota(jnp.int32, (rows, blk), 0)
    sub_c = lax.broadcasted_iota(jnp.int32, (rows, blk), 1)

    def sweep(jobs, state):
        state = dict(state)
        keys = [pl.ds(pl.multiple_of(kb * blk, blk), blk) for _, kb, _ in jobs]
        z = [_dot_nt(q_ref[0, r0:r0 + rows, :], kb_scr[ks, :]) * scale + bias for (r0, _, _), ks in zip(jobs, keys)]
        sp = [_softplus_bits(zz) for zz in z]
        vis = [sub_c < sub_r + (r0 % blk) if m else None for r0, _, m in jobs]
        nlk = [s_ if v_ is None else jnp.where(v_, s_, 0.0) for s_, v_ in zip(sp, vis)]
        later = [_dot(jnp.concatenate(_split2(l_), axis=1), upper2) for l_ in nlk]
        for (r0, _, _), ks, zz, s_, v_, l_, lt in zip(jobs, keys, z, sp, vis, nlk, later):
            tail, o = state[r0]
            a = jnp.exp2((jnp.concatenate([tail] * (blk // LANES_V7X), axis=1) + lt) + (zz - s_))
            if v_ is not None:
                a = jnp.where(v_, a, 0.0)
            o = o + _dot(a.astype(BF16), vb_scr[ks, :])
            tail = tail + jnp.broadcast_to(lt[:, 0:1] - l_[:, 0:1], tail.shape)
            state[r0] = (tail, o)
        return state

    zero = (jnp.zeros((rows, LANES_V7X), F32), jnp.zeros((rows, hd), F32))
    rows_a = [s * rows for s in range(n_sub)]
    rows_b = [blk + s * rows for s in range(n_sub)]
    all_rows = rows_a + rows_b
    state = sweep([(r, 2 * i + 1, True) for r in rows_b] + [(r, 2 * i, True) for r in rows_a]
                  + [(r, 2 * i, False) for r in rows_b], {r: zero for r in all_rows})

    def step(jj, carry):
        kb = 2 * i - 1 - jj
        state = sweep([(r, kb, False) for r in all_rows], dict(zip(all_rows, carry)))
        return tuple(state[r] for r in all_rows)

    final = lax.fori_loop(0, 2 * i, step, tuple(state[r] for r in all_rows))
    for r, (_, o) in zip(all_rows, final):
        o_ref[0, r:r + rows, :] = o.astype(o_ref.dtype)


def _sb_prompt(q, k, v, bias, *, heads):
    bsz, seq, d = q.shape
    hd = d // heads
    tq = 2 * SB_BLOCK
    assert hd == LANES_V7X and seq % tq == 0
    body = functools.partial(_sb_prompt_body, scale=hd ** -0.5)
    return pl.pallas_call(
        body,
        grid=(bsz, heads, seq // tq),
        in_specs=[
            pl.BlockSpec(memory_space=pltpu.SMEM),
            pl.BlockSpec((1, tq, hd), lambda b, h, i: (b, i, h)),
            pl.BlockSpec((1, seq, hd), lambda b, h, i: (b, 0, h)),
            pl.BlockSpec((1, seq, hd), lambda b, h, i: (b, 0, h)),
        ],
        out_specs=pl.BlockSpec((1, tq, hd), lambda b, h, i: (b, i, h)),
        out_shape=jax.ShapeDtypeStruct((bsz, seq, d), BF16),
        scratch_shapes=[pltpu.VMEM((seq, hd), BF16), pltpu.VMEM((seq, hd), BF16)],
        compiler_params=_params("parallel", "parallel", "arbitrary"),
        name="sb_prompt",
    )(bias, q, k, v)


def _sb_sample_body(pt_ref, qbd_ref, bias_ref, kn_ref, vn_ref, *refs, heads, n_new, scale, pages_per_step):
    del pt_ref
    kp_refs, vp_refs = refs[:pages_per_step], refs[pages_per_step:2 * pages_per_step]
    o_ref, acc_scr, tail_scr = refs[2 * pages_per_step:]
    p = pl.program_id(1)
    page, nl = kn_ref.shape[1], qbd_ref.shape[2]
    lower = (lax.broadcasted_iota(jnp.int32, (page, page), 1) > lax.broadcasted_iota(jnp.int32, (page, page), 0)).astype(BF16)

    def blocks(kbs, vbs, vis):
        z = [_dot(kb, qbd_ref[0]) * scale + bias_ref[...] for kb in kbs]
        sp = [_softplus(zz) for zz in z]
        lk = [-s_ if vis is None else jnp.where(vis, -s_, 0.0) for s_ in sp]
        later = []
        for l_ in lk:
            lk_hi, lk_mid, lk_lo = _split3(l_)
            later.append(_dot(lower, lk_hi) + _dot(lower, lk_mid) + _dot(lower, lk_lo))
        tail = tail_scr[0:1, :]
        acc = acc_scr[...]
        for zz, s_, l_, lt, vb in zip(z, sp, lk, later, vbs):
            a = jnp.exp((tail + lt) + (zz - s_))
            if vis is not None:
                a = jnp.where(vis, a, 0.0)
            acc = acc + _dot(jnp.transpose(a).astype(BF16), vb)
            tail = tail + jnp.sum(l_, axis=0, keepdims=True)
        acc_scr[...] = acc
        tail_scr[0:1, :] = tail

    def heads_to_lanes(ref):
        by_head = jnp.swapaxes(ref[0, 0], 0, 1)
        return jnp.concatenate([by_head[h].astype(BF16) for h in range(heads)], axis=1)

    @pl.when(p == 0)
    def _():
        acc_scr[...] = jnp.zeros_like(acc_scr)
        tail_scr[...] = jnp.zeros_like(tail_scr)
        key = lax.broadcasted_iota(jnp.int32, (page, nl), 0)
        tok = lax.broadcasted_iota(jnp.int32, (page, nl), 1) // heads
        blocks([kn_ref[0].astype(BF16)], [vn_ref[0].astype(BF16)], (key < tok) & (key < n_new))

    @pl.when(p > 0)
    def _():
        blocks([heads_to_lanes(r) for r in kp_refs], [heads_to_lanes(r) for r in vp_refs], None)

    @pl.when(p == pl.num_programs(1) - 1)
    def _():
        d = acc_scr.shape[1]
        hd = d // heads
        own = lax.broadcasted_iota(jnp.int32, (heads, d), 0) == lax.broadcasted_iota(jnp.int32, (heads, d), 1) // hd
        for t in range(n_new):
            rows = acc_scr[t * heads:(t + 1) * heads, :]
            o_ref[0, t:t + 1, :] = jnp.sum(jnp.where(own, rows, 0.0), axis=0, keepdims=True).astype(o_ref.dtype)


def _sb_sample(q, k_new, v_new, cache_k, cache_v, layer, page_table, bias, *, heads):
    bd, t_new, d = q.shape
    hd = d // heads
    page = cache_k.shape[2]
    n_pages = page_table.shape[1]
    nl = LANES_V7X
    assert heads * t_new <= nl and heads % 8 == 0 and hd % LANES_V7X == 0 and t_new <= page
    q4 = q.reshape(bd, t_new, heads, hd)
    qbd = jnp.einsum("bthx,hg->bhxtg", q4, jnp.eye(heads, dtype=q.dtype)).reshape(bd, d, t_new * heads)
    qbd = jnp.pad(qbd, ((0, 0), (0, 0), (0, nl - t_new * heads))).astype(BF16)
    bias_l = jnp.pad(jnp.tile(bias, t_new), (0, nl - t_new * heads)).reshape(1, nl)
    kn = jnp.pad(k_new, ((0, 0), (0, page - t_new), (0, 0)))
    vn = jnp.pad(v_new, ((0, 0), (0, page - t_new), (0, 0)))

    pps = SB_PAGES_PER_STEP
    assert n_pages % pps == 0

    def page_map(slot):
        return lambda b, p, pt: (layer, pt[b, n_pages - pps * jnp.maximum(p, 1) + (pps - 1 - slot)], 0, 0, 0)

    page_specs = [pl.BlockSpec((1, 1, page, heads, hd), page_map(slot)) for slot in range(pps)]
    body = functools.partial(_sb_sample_body, heads=heads, n_new=t_new, scale=hd ** -0.5, pages_per_step=pps)
    grid_spec = pltpu.PrefetchScalarGridSpec(
        num_scalar_prefetch=1,
        grid=(bd, n_pages // pps + 1),
        in_specs=[
            pl.BlockSpec((1, d, nl), lambda b, p, pt: (b, 0, 0)),
            pl.BlockSpec((1, nl), lambda b, p, pt: (0, 0)),
            pl.BlockSpec((1, page, d), lambda b, p, pt: (b, 0, 0)),
            pl.BlockSpec((1, page, d), lambda b, p, pt: (b, 0, 0)),
        ] + page_specs + page_specs,
        out_specs=pl.BlockSpec((1, t_new, d), lambda b, p, pt: (b, 0, 0)),
        scratch_shapes=[pltpu.VMEM((nl, d), F32), pltpu.VMEM((8, nl), F32)],
    )
    return pl.pallas_call(
        body,
        grid_spec=grid_spec,
        out_shape=jax.ShapeDtypeStruct((bd, t_new, d), F32),
        compiler_params=_params("parallel", "arbitrary"),
        name="sb_sample",
    )(page_table, qbd, bias_l, kn, vn, *([cache_k] * pps), *([cache_v] * pps))


def _conv_prompt_body(x_ref, xs_ref, wb_ref, wc_ref, wh_ref, wconv_ref, buf0_ref, y_ref, st_ref, sb_ref, sc_ref, sh_ref,
                      u_scr, *, blocks_per_seq):
    m = pl.program_id(1)
    tm = x_ref.shape[0]
    x = x_ref[...]
    wb, wc, wh = wb_ref[0].astype(BF16), wc_ref[0].astype(BF16), wh_ref[0].astype(BF16)
    bg = _dot(x, wb)
    u = _dot(x, wc) * _dot(x, wh)

    @pl.when(m % blocks_per_seq == 0)
    def _():
        u_scr[6:8, :] = buf0_ref[0]

    @pl.when(m == 0)
    def _():
        xs = xs_ref[...]
        sb_ref[...] = _dot(xs, wb)
        sc_ref[...] = _dot(xs, wc)
        sh_ref[...] = _dot(xs, wh)

    u_scr[8:8 + tm, :] = u
    wconv = wconv_ref[0]
    conv = wconv[0:1, :] * u_scr[6:6 + tm, :] + wconv[1:2, :] * u_scr[7:7 + tm, :] + wconv[2:3, :] * u
    y_ref[...] = (bg * conv).astype(y_ref.dtype)
    st_ref[0] = u_scr[tm + 6:tm + 8, :]
    u_scr[0:8, :] = u_scr[tm:tm + 8, :]


def _conv_prompt(xn, side, w_in, w_conv, layer, buf0, seq):
    m, d = xn.shape
    ms = side.shape[0]
    tm = _tile(seq, 1024)
    tn = _tile(d, 256)
    nb = d // tn
    blocks_per_seq = seq // tm
    body = functools.partial(_conv_prompt_body, blocks_per_seq=blocks_per_seq)
    side_spec = pl.BlockSpec((ms, tn), lambda n, i: (0, n))
    side_shape = jax.ShapeDtypeStruct((ms, d), F32)
    return pl.pallas_call(
        body,
        grid=(nb, m // tm),
        in_specs=[
            pl.BlockSpec((tm, d), lambda n, i: (i, 0)),
            pl.BlockSpec((ms, d), lambda n, i: (0, 0)),
            pl.BlockSpec((1, d, tn), lambda n, i: (layer, 0, n)),
            pl.BlockSpec((1, d, tn), lambda n, i: (layer, 0, nb + n)),
            pl.BlockSpec((1, d, tn), lambda n, i: (layer, 0, 2 * nb + n)),
            pl.BlockSpec((1, w_conv.shape[1], tn), lambda n, i: (layer, 0, n)),
            pl.BlockSpec((1, 2, tn), lambda n, i: (i // blocks_per_seq, 0, n)),
        ],
        out_specs=[
            pl.BlockSpec((tm, tn), lambda n, i: (i, n)),
            pl.BlockSpec((1, 2, tn), lambda n, i: (i // blocks_per_seq, 0, n)),
            side_spec, side_spec, side_spec,
        ],
        out_shape=[
            jax.ShapeDtypeStruct((m, d), BF16),
            jax.ShapeDtypeStruct((m // seq, 2, d), F32),
            side_shape, side_shape, side_shape,
        ],
        scratch_shapes=[pltpu.VMEM((tm + 8, tn), F32)],
        compiler_params=_params("arbitrary", "arbitrary"),
        name="conv_prompt",
    )(xn, side, w_in, w_in, w_in, w_conv, buf0)


def _conv_sample_body(bg_ref, cg_ref, h_ref, buf0_ref, wconv_ref, y_ref, st_ref, *, seq):
    taps = [buf0_ref[0], buf0_ref[1]]
    for t in range(seq):
        taps.append(cg_ref[t] * h_ref[t])
    for t in range(seq):
        conv = wconv_ref[0:1, :] * taps[t] + wconv_ref[1:2, :] * taps[t + 1] + wconv_ref[2:3, :] * taps[t + 2]
        y_ref[t] = (bg_ref[t] * conv).astype(y_ref.dtype)
    st_ref[0] = taps[seq]
    st_ref[1] = taps[seq + 1]


def _conv_sample(bg_tm, cg_tm, h_tm, buf0_tm, w_conv):
    seq, bd, d = bg_tm.shape
    return pl.pallas_call(
        functools.partial(_conv_sample_body, seq=seq),
        out_shape=[jax.ShapeDtypeStruct((seq, bd, d), BF16), jax.ShapeDtypeStruct((2, bd, d), F32)],
        compiler_params=pltpu.CompilerParams(vmem_limit_bytes=VMEM_LIMIT_V7X),
        name="conv_sample",
    )(bg_tm, cg_tm, h_tm, buf0_tm, w_conv)


def kernel(x_prompt, x_sample, state_gla, cache_sb_k, cache_sb_v, state_conv, cache_mem_k, cache_mem_v, page_table, mem_prompt, g_mix, g_xattn, g_mem, g_ffn, g_final, w_gla_in, w_gla_gate, b_gla_gate, g_gla_head, w_gla_out, w_sb_in, b_sb, w_sb_out, w_conv_in, w_conv, w_conv_out, w_xq, w_xkv, w_xo, w_ffn_in, w_ffn_out):
    bsz, seq, d = x_prompt.shape
    bd, t_new, _ = x_sample.shape
    depth = g_mix.shape[0]
    n_mem = mem_prompt.shape[1]
    gla_heads = state_gla.shape[2]
    dk = w_gla_gate.shape[2]
    dv = w_gla_out.shape[1]
    sb_heads = b_sb.shape[1]
    x_heads = cache_mem_k.shape[3]
    xw = w_xq.shape[2]
    mp, ms = bsz * seq, bd * t_new

    bf = lambda w: w.astype(BF16)
    rank = w_gla_gate.shape[1]
    w_gla_g1 = bf(jnp.pad(w_gla_in[:, :, 2 * dk + 2 * dv:], ((0, 0), (0, 0), (0, GLA_GATE_PAD - rank))))
    w_gla_gate_p = bf(jnp.pad(w_gla_gate, ((0, 0), (0, GLA_GATE_PAD - rank), (0, 0))))
    w_xq_bf, w_xkv_bf, w_xo_bf = bf(w_xq), bf(w_xkv), bf(w_xo)
    b_gla_gate3 = b_gla_gate.reshape(-1, 1, dk)
    g_gla_head3 = g_gla_head.reshape(-1, 1, dv // gla_heads)
    g_xattn3 = g_xattn.reshape(depth, 1, d)
    g_ffn3 = g_ffn.reshape(depth, 1, d)
    cache_mem_k3 = cache_mem_k.reshape(depth, bd * n_mem, xw)
    cache_mem_v3 = cache_mem_v.reshape(depth, bd * n_mem, xw)

    hp = x_prompt.reshape(mp, d)
    hs = x_sample.reshape(ms, d)
    mem_k, mem_v = _mem_kv(mem_prompt.reshape(bsz * n_mem, d), g_mem, w_xkv_bf)

    gla_p = jnp.zeros((state_gla.shape[0], bsz) + state_gla.shape[2:], F32)
    gla_s = jnp.zeros(state_gla.shape, F32)
    sbk_p, sbv_p, sbk_s, sbv_s, conv_p, conv_s = [], [], [], [], [], []
    for i in range(depth):
        kind, j = i % 3, i // 3
        xp = _rmsnorm(hp, g_mix[i], BF16)
        xs = _rmsnorm(hs, g_mix[i], BF16)
        if kind == 0:
            hp, hs, gla_p, gla_s = _gla_mixer(xp, xs, hp, hs, state_gla, gla_p, gla_s, j, seq, t_new, w_gla_in, w_gla_g1,
                                              w_gla_gate_p, b_gla_gate3, g_gla_head3, w_gla_out,
                                              heads=gla_heads, dk=dk, dv=dv)
        elif kind == 1:
            qp, qs = _matmul(xp, w_sb_in, j, n_out=d, col_off=0, out_dtype=BF16, side=xs, name="sb_q")
            kp, kn = _matmul(xp, w_sb_in, j, n_out=d, col_off=d, side=xs, name="sb_k")
            vp, vn = _matmul(xp, w_sb_in, j, n_out=d, col_off=2 * d, side=xs, name="sb_v")
            qs, kn, vn = (a.reshape(bd, t_new, d) for a in (qs, kn, vn))
            op = _sb_prompt(qp.reshape(bsz, seq, d), kp.reshape(bsz, seq, d), vp.reshape(bsz, seq, d), b_sb[j],
                            heads=sb_heads)
            os_ = _sb_sample(qs, kn, vn, cache_sb_k, cache_sb_v, j, page_table, b_sb[j], heads=sb_heads)
            hp, hs = _matmul(op.reshape(mp, d), w_sb_out, j, n_out=d, res=hp,
                             side=os_.reshape(ms, d).astype(BF16), side_res=hs, name="sb_out")
            sbk_p.append(kp)
            sbv_p.append(vp)
            sbk_s.append(kn)
            sbv_s.append(vn)
        else:
            yp, bp, *side_proj = _conv_prompt(xp, xs, w_conv_in, w_conv, j, jnp.zeros((bsz, 2, d), F32), seq)
            ys, bs = _conv_sample(*(a.reshape(bd, t_new, d).transpose(1, 0, 2) for a in side_proj),
                                  state_conv[j].transpose(1, 0, 2), w_conv[j])
            hp, hs = _matmul(yp, w_conv_out, j, n_out=d, res=hp,
                             side=ys.transpose(1, 0, 2).reshape(ms, d), side_res=hs, name="conv_out")
            conv_p.append(bp)
            conv_s.append(bs.transpose(1, 0, 2))
        hp, xp_ffn = _xattn(hp, g_xattn3, g_ffn3, w_xq_bf, mem_k, mem_v, w_xo_bf, i,
                            heads=x_heads, rows_per_seq=seq, n_mem=n_mem, grouped=False)
        hs, xs_ffn = _xattn(hs, g_xattn3, g_ffn3, w_xq_bf, cache_mem_k3, cache_mem_v3, w_xo_bf, i,
                            heads=x_heads, rows_per_seq=t_new, n_mem=n_mem, grouped=True)
        ap, as_, w_out_bf = _swiglu_in(xp_ffn, xs_ffn, w_ffn_in, w_ffn_out, i, tm_pref=2048, tf_pref=256)
        hp, hs = _matmul(ap, w_out_bf[None], 0, n_out=d, res=hp, side=as_, side_res=hs,
                         tm_pref=512, tn_pref=512, cols_outer=True, name="ffn_out")

    y_prompt = _rmsnorm(hp, g_final, F32).reshape(bsz, seq, d)
    y_sample = _rmsnorm(hs, g_final, F32).reshape(bd, t_new, d)
    hd = d // sb_heads
    sb_p = lambda xs_: jnp.stack(xs_).reshape(-1, bsz, seq, sb_heads, hd)
    sb_s = lambda xs_: jnp.stack(xs_).reshape(-1, bd, t_new, sb_heads, hd)
    xhd = xw // x_heads
    return (y_prompt, y_sample,
            gla_p, gla_s,
            sb_p(sbk_p), sb_p(sbv_p), sb_s(sbk_s), sb_s(sbv_s),
            jnp.stack(conv_p), jnp.stack(conv_s),
            mem_k.reshape(depth, bsz, n_mem, x_heads, xhd), mem_v.reshape(depth, bsz, n_mem, x_heads, xhd))
```
